```python
import math
import jax, jax.numpy as jnp
from jax import lax
import numpy as np

D_MODEL = 1024
BATCH = 8
SEQ = 4096
DEPTH = 2

N_META = 16
MIX_WIDTH = D_MODEL
ATTN_WIDTH = MIX_WIDTH // 2
HGRN_WIDTH = MIX_WIDTH - ATTN_WIDTH
ATTN_HEADS = 8
ATTN_HEAD_DIM = ATTN_WIDTH // ATTN_HEADS
KV_RANK = 128
IDX_HEADS = 4
IDX_DIM = 64
TOPK_MAX = 256
Q_BLOCK = 128
HGRN_EXPAND = 128
HGRN_HEADS = HGRN_WIDTH // HGRN_EXPAND
HGRN_CHUNK = 64
REL_BUCKETS = 32
REL_MAX_DIST = 128
DN_ALPHA = (2 * DEPTH) ** 0.25
DN_BETA = (8 * DEPTH) ** -0.25
EPS = 1e-6

IN_SIZES = (ATTN_WIDTH, KV_RANK, IDX_HEADS * IDX_DIM, IDX_DIM, IDX_HEADS, ATTN_WIDTH,
            HGRN_WIDTH, HGRN_WIDTH, HGRN_WIDTH, HGRN_WIDTH)
N_IN = sum(IN_SIZES)
IN_OFFSETS = tuple(int(v) for v in np.cumsum(IN_SIZES)[:-1])

kernel_name = "hymba_dsa_hgrn2_deepnorm"


def layer_norm(x, g, b):
    xf = x.astype(jnp.float32)
    mu = jnp.mean(xf, axis=-1, keepdims=True)
    var = jnp.mean(jnp.square(xf - mu), axis=-1, keepdims=True)
    return ((xf - mu) * lax.rsqrt(var + EPS) * g + b).astype(x.dtype)


def rms_norm(x, g):
    xf = x.astype(jnp.float32)
    return (xf * lax.rsqrt(jnp.mean(jnp.square(xf), axis=-1, keepdims=True) + EPS) * g).astype(x.dtype)


def t5_bucket(dist):
    n = jnp.maximum(dist, 0)
    max_exact = REL_BUCKETS // 2
    nf = jnp.maximum(n, 1).astype(jnp.float32)
    large = max_exact + (jnp.log(nf / max_exact) / math.log(REL_MAX_DIST / max_exact)
                         * (REL_BUCKETS - max_exact)).astype(jnp.int32)
    large = jnp.minimum(large, REL_BUCKETS - 1)
    return jnp.where(n < max_exact, n, large)


def dsa_attention(q, c, qi, ki, wi, rel_bias, w_uk, w_uv):
    B, E = c.shape[0], c.shape[1]
    S = E - N_META
    k_top = min(TOPK_MAX, S // 4)
    scale = ATTN_HEAD_DIM ** -0.5
    q_lat = jnp.einsum('bthd,hdc->bthc', q, w_uk)
    c_m, c_r = c[:, :N_META], c[:, N_META:]
    ql_m, ql_r = q_lat[:, :N_META], q_lat[:, N_META:]
    qi_r, ki_r = qi[:, N_META:], ki[:, N_META:]
    wi_r = wi[:, N_META:] * (IDX_HEADS ** -0.5)
    meta_pos = jnp.arange(N_META)
    key_pos = jnp.arange(S)
    b_idx = jnp.arange(B)[:, None, None]

    lg = jnp.einsum('bqhc,bmc->bqhm', ql_m, c_m).astype(jnp.float32) * scale
    bm = rel_bias[t5_bucket(meta_pos[:, None] - meta_pos[None, :])]
    lg = lg + jnp.transpose(bm, (0, 2, 1))[None].astype(jnp.float32)
    lg = jnp.where((meta_pos[:, None] >= meta_pos[None, :])[None, :, None, :], lg, -jnp.inf)
    o_m = jnp.einsum('bqhm,bmc->bqhc', jax.nn.softmax(lg, axis=-1).astype(c.dtype), c_m)

    def block(bi):
        start = bi * Q_BLOCK
        q_pos = start + jnp.arange(Q_BLOCK)
        ql = lax.dynamic_slice_in_dim(ql_r, start, Q_BLOCK, axis=1)
        qib = lax.dynamic_slice_in_dim(qi_r, start, Q_BLOCK, axis=1)
        wib = lax.dynamic_slice_in_dim(wi_r, start, Q_BLOCK, axis=1)
        s_h = jax.nn.relu(jnp.einsum('bqhd,bkd->bqhk', qib, ki_r).astype(jnp.float32) * (IDX_DIM ** -0.5))
        s_idx = jnp.einsum('bqhk,bqh->bqk', s_h, wib.astype(jnp.float32))
        causal = key_pos[None, :] <= q_pos[:, None]
        s_idx = jnp.where(causal[None], s_idx, -jnp.inf)
        _, sel = lax.top_k(s_idx, k_top)
        valid = sel <= q_pos[None, :, None]
        c_sel = c_r[b_idx, sel]
        lg_s = jnp.einsum('bqhc,bqkc->bqhk', ql, c_sel).astype(jnp.float32) * scale
        bs = rel_bias[t5_bucket(q_pos[None, :, None] - sel)]
        lg_s = lg_s + jnp.moveaxis(bs, -1, 2).astype(jnp.float32)
        lg_s = jnp.where(valid[:, :, None, :], lg_s, -jnp.inf)
        lg_m = jnp.einsum('bqhc,bmc->bqhm', ql, c_m).astype(jnp.float32) * scale
        bmr = rel_bias[t5_bucket(N_META + q_pos[:, None] - meta_pos[None, :])]
        lg_m = lg_m + jnp.transpose(bmr, (0, 2, 1))[None].astype(jnp.float32)
        p = jax.nn.softmax(jnp.concatenate([lg_m, lg_s], axis=-1), axis=-1).astype(c.dtype)
        return (jnp.einsum('bqhm,bmc->bqhc', p[..., :N_META], c_m)
                + jnp.einsum('bqhk,bqkc->bqhc', p[..., N_META:], c_sel))

    o_r = lax.map(block, jnp.arange(S // Q_BLOCK))
    o_r = jnp.moveaxis(o_r, 0, 1).reshape(B, S, ATTN_HEADS, KV_RANK)
    o_lat = jnp.concatenate([o_m, o_r], axis=1)
    return jnp.einsum('bthc,hcd->bthd', o_lat, w_uv).reshape(B, E, ATTN_WIDTH)


def hgrn2(q, f_raw, i, lb):
    B, E = q.shape[0], q.shape[1]
    dt = q.dtype
    qf = jax.nn.silu(q.astype(jnp.float32)) * (HGRN_EXPAND ** -0.5)
    f = lb + (1.0 - lb) * jax.nn.sigmoid(f_raw.astype(jnp.float32))
    g = jnp.log(f)
    k = 1.0 - f
    v = i.astype(jnp.float32)
    pad = HGRN_CHUNK - N_META

    def to_chunks(a):
        a = jnp.pad(a, ((0, 0), (pad, 0), (0, 0)))
        n = a.shape[1] // HGRN_CHUNK
        return a.reshape(B, n, HGRN_CHUNK, HGRN_HEADS, HGRN_EXPAND).transpose(1, 0, 3, 2, 4)

    tri = jnp.tril(jnp.ones((HGRN_CHUNK, HGRN_CHUNK), dtype=bool))

    def step(state, inp):
        qc, kc, vc, gc = inp
        bcum = jnp.cumsum(gc, axis=2)
        o_inter = jnp.einsum('bhtk,bhkv->bhtv', qc * jnp.exp(bcum), state)
        diff = bcum[:, :, :, None, :] - bcum[:, :, None, :, :]
        decay = jnp.exp(jnp.where(tri[:, :, None], diff, -jnp.inf))
        scores = jnp.sum(qc[:, :, :, None, :] * decay * kc[:, :, None, :, :], axis=-1)
        o_intra = jnp.einsum('bhts,bhsv->bhtv', scores, vc)
        b_last = bcum[:, :, -1:, :]
        new_state = (jnp.exp(b_last[:, :, 0, :])[..., None] * state
                     + jnp.einsum('bhsk,bhsv->bhkv', kc * jnp.exp(b_last - bcum), vc))
        return new_state, o_inter + o_intra

    s0 = jnp.zeros((B, HGRN_HEADS, HGRN_EXPAND, HGRN_EXPAND), jnp.float32)
    _, outs = lax.scan(step, s0, (to_chunks(qf), to_chunks(k), to_chunks(v), to_chunks(g)))
    n = outs.shape[0]
    o = outs.transpose(1, 0, 3, 2, 4).reshape(B, n * HGRN_CHUNK, HGRN_HEADS, HGRN_EXPAND)[:, pad:]
    return o.astype(dt)


def setup_inputs(seed: int = 0) -> dict:
    key = jax.random.key(seed)
    ks = jax.random.split(key, 12)
    nrm = jax.random.normal
    x = nrm(ks[0], (BATCH, SEQ, D_MODEL), jnp.float32)
    meta_tokens = nrm(ks[1], (N_META, D_MODEL), jnp.float32)
    rel_bias = 0.5 * nrm(ks[2], (REL_BUCKETS, ATTN_HEADS), jnp.float32)
    hgrn_lb_raw = nrm(ks[3], (DEPTH, HGRN_WIDTH), jnp.float32)
    w_in = nrm(ks[4], (DEPTH, D_MODEL, N_IN), jnp.float32) * D_MODEL ** -0.5
    kv_norm_g = 1.0 + 0.1 * nrm(ks[5], (DEPTH, KV_RANK), jnp.float32)
    w_uk = nrm(ks[6], (DEPTH, ATTN_HEADS, ATTN_HEAD_DIM, KV_RANK), jnp.float32) * ATTN_HEAD_DIM ** -0.5
    w_uv = nrm(ks[7], (DEPTH, ATTN_HEADS, KV_RANK, ATTN_HEAD_DIM), jnp.float32) * (KV_RANK ** -0.5 * DN_BETA)
    hgrn_norm_g = 1.0 + 0.1 * nrm(ks[8], (DEPTH, HGRN_EXPAND), jnp.float32)
    w_out = nrm(ks[9], (DEPTH, MIX_WIDTH, D_MODEL), jnp.float32) * (MIX_WIDTH ** -0.5 * DN_BETA)
    ln_g = 1.0 + 0.1 * nrm(ks[10], (DEPTH, D_MODEL), jnp.float32)
    ln_b = 0.02 * nrm(ks[11], (DEPTH, D_MODEL), jnp.float32)
    return {"x": x, "meta_tokens": meta_tokens, "rel_bias": rel_bias, "hgrn_lb_raw": hgrn_lb_raw,
            "w_in": w_in, "kv_norm_g": kv_norm_g, "w_uk": w_uk, "w_uv": w_uv,
            "hgrn_norm_g": hgrn_norm_g, "w_out": w_out, "ln_g": ln_g, "ln_b": ln_b}


def reference(x, meta_tokens, rel_bias, hgrn_lb_raw, w_in, kv_norm_g, w_uk, w_uv,
              hgrn_norm_g, w_out, ln_g, ln_b):
    B = x.shape[0]
    h = jnp.concatenate([jnp.broadcast_to(meta_tokens[None].astype(x.dtype), (B, N_META, D_MODEL)), x], axis=1)
    E = h.shape[1]
    lb_p = jax.nn.softmax(hgrn_lb_raw.astype(jnp.float32), axis=0)
    lb_all = jnp.cumsum(lb_p, axis=0) - lb_p[0:1]
    for l in range(DEPTH):
        proj = h @ w_in[l]
        (q_a, c_kv, q_idx, k_idx, w_idx, gate_a,
         q_h, f_h, i_h, gate_h) = jnp.split(proj, IN_OFFSETS, axis=-1)
        a = dsa_attention(q_a.reshape(B, E, ATTN_HEADS, ATTN_HEAD_DIM), rms_norm(c_kv, kv_norm_g[l]),
                          q_idx.reshape(B, E, IDX_HEADS, IDX_DIM), k_idx, w_idx,
                          rel_bias, w_uk[l], w_uv[l])
        a = a * jax.nn.silu(gate_a)
        r = hgrn2(q_h, f_h, i_h, lb_all[l])
        r = rms_norm(r, hgrn_norm_g[l]).reshape(B, E, HGRN_WIDTH) * jax.nn.silu(gate_h)
        y = jnp.concatenate([a, r], axis=-1) @ w_out[l]
        h = layer_norm(DN_ALPHA * h + y, ln_g[l], ln_b[l])
    return h[:, N_META:]
```

```python
import functools
import math

import numpy as np
import jax
import jax.numpy as jnp
from jax import lax
from jax.experimental import pallas as pl
from jax.experimental.pallas import tpu as pltpu

D_MODEL = 1024
DEPTH = 2
N_META = 16
ATTN_WIDTH = 512
HGRN_WIDTH = 512
ATTN_HEADS = 8
ATTN_HEAD_DIM = 64
KV_RANK = 128
IDX_HEADS = 4
IDX_DIM = 64
TOPK_MAX = 256
HGRN_EXPAND = 128
HGRN_HEADS = 4
REL_BUCKETS = 32
REL_MAX_DIST = 128
DN_ALPHA = (2 * DEPTH) ** 0.25
EPS = 1e-6

F32 = jnp.float32
BF16 = jnp.bfloat16

LANES = 128
QB = 128
HGRN_CHUNK = 128
PROJ_ROWS = 256
VMEM_LIMIT = 48 * 1024 * 1024

_P_QA, _P_CKV, _P_QI, _P_KW, _P_GA, _P_QH, _P_FH, _P_IH, _P_GH, _P_END = (
    0, 512, 640, 896, 1024, 1536, 2048, 2560, 3072, 3584)
_RAW_SPLIT = 964

NEG_INF_KEY = -2139095041
INT_MIN = -2147483648


def _bucket_starts():
    max_exact = REL_BUCKETS // 2
    n = np.arange(0, 4 * REL_MAX_DIST)
    nf = np.maximum(n, 1).astype(np.float64)
    large = max_exact + (np.log(nf / max_exact) / math.log(REL_MAX_DIST / max_exact)
                         * (REL_BUCKETS - max_exact)).astype(np.int64)
    large = np.minimum(large, REL_BUCKETS - 1)
    bucket = np.where(n < max_exact, n, large)
    return [int(np.argmax(bucket >= b)) for b in range(REL_BUCKETS)]


_BUCKET_START = _bucket_starts()


def _dot(a, b):
    return jnp.dot(a, b, preferred_element_type=F32)


def _dot_nt(a, b):
    return lax.dot_general(a, b, (((1,), (1,)), ((), ())), preferred_element_type=F32)


def _dot_tn(a, b):
    return lax.dot_general(a, b, (((0,), (0,)), ((), ())), preferred_element_type=F32)


def _sigmoid(x):
    return 1.0 / (1.0 + jnp.exp(-x))


def _params(*sem):
    return pltpu.CompilerParams(dimension_semantics=sem, vmem_limit_bytes=VMEM_LIMIT)


def _proj_kernel(layer, h_ref, wp_ref, wuk_ref, kvg_ref, lbraw_ref,
                 ql_ref, cb_ref, qi_ref, kib_ref, kw_ref, ga_ref,
                 qf_ref, g_ref, kk_ref, v_ref, gh_ref):
    hb = h_ref[...].astype(BF16)

    def proj(lo, hi):
        return _dot(hb, wp_ref[:, lo:hi])

    ql = _dot(proj(_P_QA, _P_CKV).astype(BF16), wuk_ref[...]) * (ATTN_HEAD_DIM ** -0.5)
    for h in range(ATTN_HEADS):
        ql_ref[h] = ql[:, h * KV_RANK:(h + 1) * KV_RANK].astype(BF16)

    ckv = proj(_P_CKV, _P_QI)
    c = ckv * lax.rsqrt(jnp.mean(ckv * ckv, axis=-1, keepdims=True) + EPS) * kvg_ref[...]
    cb_ref[...] = c.astype(BF16)

    qi_ref[...] = proj(_P_QI, _P_KW).astype(BF16)
    kw = proj(_P_KW, _P_GA)
    kib_ref[...] = kw[:, :IDX_DIM].astype(BF16)
    kw_ref[...] = kw * (IDX_HEADS ** -0.5 * IDX_DIM ** -0.5)

    ga = proj(_P_GA, _P_QH)
    ga_ref[...] = ga * _sigmoid(ga)

    qh = proj(_P_QH, _P_FH)
    qf_ref[...] = qh * _sigmoid(qh) * (HGRN_EXPAND ** -0.5)

    raw = lbraw_ref[...]
    ex = jnp.exp(raw - jnp.max(raw, axis=0, keepdims=True))
    lbp = ex / jnp.sum(ex, axis=0, keepdims=True)
    lb = lbp[0:1]
    for i in range(1, layer + 1):
        lb = lb + lbp[i:i + 1]
    lb = lb - lbp[0:1]
    f = lb + (1.0 - lb) * _sigmoid(proj(_P_FH, _P_IH))
    g_ref[...] = jnp.log(f)
    kk_ref[...] = 1.0 - f
    v_ref[...] = proj(_P_IH, _P_GH)
    gh = proj(_P_GH, _P_END)
    gh_ref[...] = gh * _sigmoid(gh)


def _project(layer, h2d, wp, wuk_bd, kvg, lbraw, rows):
    t = h2d.shape[0]
    grid = (t // rows,)
    row_spec = lambda w: pl.BlockSpec((rows, w), lambda r: (r, 0))
    full = lambda a: pl.BlockSpec(a.shape, lambda r: (0,) * a.ndim)
    out_shape = (
        jax.ShapeDtypeStruct((ATTN_HEADS, t, KV_RANK), BF16),
        jax.ShapeDtypeStruct((t, KV_RANK), BF16),
        jax.ShapeDtypeStruct((t, IDX_HEADS * IDX_DIM), BF16),
        jax.ShapeDtypeStruct((t, IDX_DIM), BF16),
        jax.ShapeDtypeStruct((t, LANES), F32),
        jax.ShapeDtypeStruct((t, ATTN_WIDTH), F32),
        jax.ShapeDtypeStruct((t, HGRN_WIDTH), F32),
        jax.ShapeDtypeStruct((t, HGRN_WIDTH), F32),
        jax.ShapeDtypeStruct((t, HGRN_WIDTH), F32),
        jax.ShapeDtypeStruct((t, HGRN_WIDTH), F32),
        jax.ShapeDtypeStruct((t, HGRN_WIDTH), F32),
    )
    out_specs = (
        pl.BlockSpec((ATTN_HEADS, rows, KV_RANK), lambda r: (0, r, 0)),
        row_spec(KV_RANK), row_spec(IDX_HEADS * IDX_DIM), row_spec(IDX_DIM), row_spec(LANES),
        row_spec(ATTN_WIDTH), row_spec(HGRN_WIDTH), row_spec(HGRN_WIDTH), row_spec(HGRN_WIDTH),
        row_spec(HGRN_WIDTH), row_spec(HGRN_WIDTH),
    )
    return pl.pallas_call(
        functools.partial(_proj_kernel, layer),
        grid=grid,
        in_specs=[row_spec(D_MODEL), full(wp), full(wuk_bd), full(kvg), full(lbraw)],
        out_specs=out_specs,
        out_shape=out_shape,
        compiler_params=_params("arbitrary"),
        name=f"proj_l{layer}_r{rows}",
    )(h2d, wp, wuk_bd, kvg, lbraw)


def _bias_tile(dist, rb_ref, h):
    out = jnp.full(dist.shape, rb_ref[0, h], F32)
    for b in range(1, REL_BUCKETS):
        out = jnp.where(dist >= _BUCKET_START[b], rb_ref[b, h], out)
    return out


def _bias_kernel(rb_ref, tbd_ref, tbp_ref, tm0_ref):
    q = lax.broadcasted_iota(jnp.int32, (QB, QB), 0)
    k = lax.broadcasted_iota(jnp.int32, (QB, QB), 1)
    for h in range(ATTN_HEADS):
        tbd_ref[h] = _bias_tile(jnp.maximum(q - k, 0), rb_ref, h)
        tbp_ref[h] = _bias_tile(q - k + QB, rb_ref, h)
        tm0_ref[h] = _bias_tile(jnp.maximum(N_META + q - k, 0), rb_ref, h)


def _bias_tables(rel_bias):
    shp = jax.ShapeDtypeStruct((ATTN_HEADS, QB, QB), F32)
    return pl.pallas_call(
        _bias_kernel,
        in_specs=[pl.BlockSpec(memory_space=pltpu.SMEM)],
        out_shape=(shp, shp, shp),
        name="bias_tables",
    )(rel_bias)


def _to_key(s):
    bits = pltpu.bitcast(s, jnp.int32)
    return jnp.where(bits < 0, bits ^ 0x7FFFFFFF, bits)


def _attn_kernel(k_top, qi_ref, kib_ref, kw_ref, ql_ref, cb_ref, cm_ref, ga_ref,
                 tbd_ref, tbp_ref, tm0_ref, rb_ref, wuv_ref, a_ref,
                 key_scr, am_scr, lg_scr, o_scr):
    i = pl.program_id(1)
    nblk = i + 1
    q_loc = lax.broadcasted_iota(jnp.int32, (QB, QB), 0)
    k_loc = lax.broadcasted_iota(jnp.int32, (QB, QB), 1)

    qi = qi_ref[...]
    qis = [qi[:, h * IDX_DIM:(h + 1) * IDX_DIM] for h in range(IDX_HEADS)]
    wq = kw_ref[pl.ds(pl.multiple_of(i * QB, QB), QB), :]
    wbs = [jnp.broadcast_to(wq[:, IDX_DIM + h:IDX_DIM + h + 1], (QB, QB)) for h in range(IDX_HEADS)]

    def scores(j):
        kj = kib_ref[pl.ds(pl.multiple_of(j * QB, QB), QB), :]
        s = jnp.maximum(_dot_nt(qis[0], kj), 0.0) * wbs[0]
        for h in range(1, IDX_HEADS):
            s = s + jnp.maximum(_dot_nt(qis[h], kj), 0.0) * wbs[h]
        return s

    def score_body(j, carry):
        key_scr[j] = _to_key(scores(j))
        return carry

    lax.fori_loop(0, i, score_body, 0)
    key_scr[i] = _to_key(jnp.where(k_loc <= q_loc, scores(i), -jnp.inf))

    def count_ge(cand):
        def body(j, acc):
            return acc + jnp.where(key_scr[j] >= cand, 1.0, 0.0)
        acc = lax.fori_loop(0, nblk, body, jnp.zeros((QB, QB), F32))
        return jnp.sum(acc, axis=1, keepdims=True)

    kf = float(k_top)
    zero = jnp.zeros((QB, QB), jnp.int32)
    prefix = jnp.where(count_ge(zero) >= kf, zero, jnp.full((QB, QB), INT_MIN, jnp.int32))

    def bit_body(t, prefix):
        cand = prefix | jnp.left_shift(jnp.int32(1), 30 - t)
        return jnp.where(count_ge(cand) >= kf, cand, prefix)

    thr = lax.fori_loop(0, 31, bit_body, prefix)

    def count_gt_body(j, acc):
        return acc + jnp.where(key_scr[j] > thr, 1.0, 0.0)

    n_gt = jnp.sum(lax.fori_loop(0, nblk, count_gt_body, jnp.zeros((QB, QB), F32)),
                   axis=1, keepdims=True)
    need = jnp.where(thr > NEG_INF_KEY, kf - n_gt, 0.0)
    thr_valid = jnp.maximum(thr, NEG_INF_KEY)
    incl = jnp.where(q_loc <= k_loc, 1.0, 0.0).astype(BF16)

    def sel_body(j, run):
        key = key_scr[j]
        eq = key == thr
        pre = _dot(jnp.where(eq, 1.0, 0.0).astype(BF16), incl)
        rank = run + pre
        tie = jnp.where(eq, jnp.where(rank <= need, 0.0, -jnp.inf), -jnp.inf)
        am_scr[j] = jnp.where(key > thr_valid, 0.0, tie)
        return run + pre[:, QB - 1:QB]

    lax.fori_loop(0, nblk, sel_body, jnp.zeros((QB, 1), F32))

    cm = cm_ref[...]
    has_prev = jnp.where(i >= 1, 0.0, -jnp.inf)
    jp = jnp.maximum(i - 1, 0)

    def head_body(h, carry):
        qh = ql_ref[h]
        bfar = rb_ref[REL_BUCKETS - 1, h]

        def kblk(j):
            return cb_ref[pl.ds(pl.multiple_of(j * QB, QB), QB), :]

        def far_body(j, m):
            lg = _dot_nt(qh, kblk(j)) + bfar + am_scr[j]
            lg_scr[j] = lg
            return jnp.maximum(m, lg)

        m = lax.fori_loop(0, jp, far_body, jnp.full((QB, QB), -jnp.inf, F32))
        lgp = _dot_nt(qh, kblk(jp)) + tbp_ref[h] + am_scr[jp] + has_prev
        lgd = _dot_nt(qh, kblk(i)) + tbd_ref[h] + am_scr[i]
        lgm = _dot_nt(qh, cm) + jnp.where(i == 0, tm0_ref[h][:, :N_META], bfar)
        m = jnp.maximum(jnp.maximum(m, lgp), lgd)
        m_row = jnp.maximum(jnp.max(m, axis=1, keepdims=True), jnp.max(lgm, axis=1, keepdims=True))

        def pv_body(j, lo):
            l, o = lo
            p = jnp.exp(lg_scr[j] - m_row)
            return l + p, o + _dot(p.astype(BF16), kblk(j))

        l, o = lax.fori_loop(0, jp, pv_body,
                             (jnp.zeros((QB, QB), F32), jnp.zeros((QB, KV_RANK), F32)))
        pp = jnp.exp(lgp - m_row)
        pd = jnp.exp(lgd - m_row)
        pm = jnp.exp(lgm - m_row)
        l_row = jnp.sum(l + pp + pd, axis=1, keepdims=True) + jnp.sum(pm, axis=1, keepdims=True)
        o = (o + _dot(pp.astype(BF16), kblk(jp)) + _dot(pd.astype(BF16), kblk(i))
             + _dot(pm.astype(BF16), cm))
        o_scr[h] = (o / l_row).astype(BF16)
        return carry

    lax.fori_loop(0, ATTN_HEADS, head_body, 0)

    a = _dot(o_scr[0], wuv_ref[0])
    for h in range(1, ATTN_HEADS):
        a = a + _dot(o_scr[h], wuv_ref[h])
    a_ref[...] = a * ga_ref[...]


def _attention(k_top, batch, seq, qi, kib, kw, ql, cb, cm, ga, tbd, tbp, tm0, rel_bias, wuv_pad):
    nq = seq // QB
    full = lambda a: pl.BlockSpec(a.shape, lambda b, i: (0,) * a.ndim)
    blk = lambda w: pl.BlockSpec((QB, w), lambda b, i: (b * nq + i, 0))
    per_batch = lambda w: pl.BlockSpec((seq, w), lambda b, i: (b, 0))
    return pl.pallas_call(
        functools.partial(_attn_kernel, k_top),
        grid=(batch, nq),
        in_specs=[
            blk(IDX_HEADS * IDX_DIM),
            per_batch(IDX_DIM),
            per_batch(LANES),
            pl.BlockSpec((ATTN_HEADS, QB, KV_RANK), lambda b, i: (0, b * nq + i, 0)),
            per_batch(KV_RANK),
            full(cm),
            blk(ATTN_WIDTH),
            full(tbd), full(tbp), full(tm0),
            pl.BlockSpec(memory_space=pltpu.SMEM),
            full(wuv_pad),
        ],
        out_specs=blk(ATTN_WIDTH),
        out_shape=jax.ShapeDtypeStruct((batch * seq, ATTN_WIDTH), F32),
        scratch_shapes=[
            pltpu.VMEM((nq, QB, QB), jnp.int32),
            pltpu.VMEM((nq, QB, QB), F32),
            pltpu.VMEM((nq, QB, QB), F32),
            pltpu.VMEM((ATTN_HEADS, QB, KV_RANK), BF16),
        ],
        compiler_params=_params("arbitrary", "arbitrary"),
        name="dsa_attention",
    )(qi, kib, kw, ql, cb, cm, ga, tbd, tbp, tm0, rel_bias, wuv_pad)


def _meta_attn_kernel(ql_ref, cm_ref, ga_ref, tbd_ref, wuv_ref, a_ref):
    cm = cm_ref[...]
    q = lax.broadcasted_iota(jnp.int32, (N_META, N_META), 0)
    k = lax.broadcasted_iota(jnp.int32, (N_META, N_META), 1)
    a = jnp.zeros((N_META, ATTN_WIDTH), F32)
    for h in range(ATTN_HEADS):
        lg = _dot_nt(ql_ref[h], cm) + tbd_ref[h][:N_META, :N_META]
        lg = jnp.where(q >= k, lg, -jnp.inf)
        p = jnp.exp(lg - jnp.max(lg, axis=1, keepdims=True))
        p = p / jnp.sum(p, axis=1, keepdims=True)
        o = _dot(p.astype(BF16), cm)
        a = a + _dot(o.astype(BF16), wuv_ref[h])
    a_ref[...] = a * ga_ref[...]


def _meta_attention(ql_m, cm, ga_m, tbd, wuv_pad):
    return pl.pallas_call(
        _meta_attn_kernel,
        out_shape=jax.ShapeDtypeStruct((N_META, ATTN_WIDTH), F32),
        name="meta_attention",
    )(ql_m, cm, ga_m, tbd, wuv_pad)


def _hgrn_levels(chunk):
    return [1 << p for p in range(int(math.log2(chunk)))]


def _hgrn_masks(chunk):
    t = np.arange(chunk)[:, None]
    s = np.arange(chunk)[None, :]
    masks = [t == s]
    for m in _hgrn_levels(chunk):
        same = (t // (2 * m)) == (s // (2 * m))
        masks.append(same & ((t & m) != 0) & ((s & m) == 0))
    return np.stack(masks).astype(np.float32)


def _block_ref_rows(b, m, chunk, row):
    if 2 * m >= 8:
        pieces = []
        for blk in range(chunk // (2 * m)):
            r = blk * 2 * m + m - 1
            pieces.append(jnp.broadcast_to(b[r:r + 1, :], (2 * m, b.shape[1])))
        return pieces[0] if len(pieces) == 1 else jnp.concatenate(pieces, axis=0)
    delta = (row & (2 * m - 1)) - (m - 1)
    out = b
    for d in range(-(m - 1), m + 1):
        if d != 0:
            out = jnp.where(delta == d, pltpu.roll(b, d % chunk, 0), out)
    return out


def _split3(x):
    hi = x.astype(BF16)
    r1 = x - hi.astype(F32)
    mid = r1.astype(BF16)
    lo = (r1 - mid.astype(F32)).astype(BF16)
    return hi, mid, lo


def _hgrn_kernel(chunk, qf_ref, g_ref, kk_ref, v_ref, gh_ref, s0_ref, ng_ref, tri_ref, lm_ref,
                 r_ref, sfin_ref, st_scr):
    ci = pl.program_id(1)

    @pl.when(ci == 0)
    def _():
        st_scr[...] = s0_ref[...]

    tri = tri_ref[...]
    row = lax.broadcasted_iota(jnp.int32, (chunk, HGRN_EXPAND), 0)
    levels = _hgrn_levels(chunk)
    for hd in range(HGRN_HEADS):
        sl = slice(hd * HGRN_EXPAND, (hd + 1) * HGRN_EXPAND)
        q = qf_ref[:, sl]
        g = g_ref[:, sl]
        k = kk_ref[:, sl]
        vb = v_ref[:, sl].astype(BF16)
        g_hi, g_mid, g_lo = _split3(g)
        b = _dot(tri, g_hi) + _dot(tri, g_mid) + _dot(tri, g_lo)

        a = _dot_nt(q.astype(BF16), k.astype(BF16)) * lm_ref[0]
        for li, m in enumerate(levels):
            bref = _block_ref_rows(b, m, chunk, row)
            e = jnp.exp(jnp.where((row & m) != 0, b - bref, bref - b))
            a = a + _dot_nt((q * e).astype(BF16), (k * e).astype(BF16)) * lm_ref[li + 1]

        st = st_scr[hd]
        o = _dot_nt((q * jnp.exp(b)).astype(BF16), st.astype(BF16)) + _dot(a.astype(BF16), vb)
        b_last = b[chunk - 1:chunk, :]
        kd = (k * jnp.exp(b_last - b)).astype(BF16)
        st_scr[hd] = st * jnp.exp(b_last) + _dot_tn(vb, kd)

        rn = o * lax.rsqrt(jnp.mean(o * o, axis=-1, keepdims=True) + EPS) * ng_ref[...]
        r_ref[:, sl] = rn * gh_ref[:, sl]

    @pl.when(ci == pl.num_programs(1) - 1)
    def _():
        sfin_ref[0] = st_scr[...]


def _hgrn(batch, seq, chunk, qf, g, kk, v, gh, s0, ng):
    nc = seq // chunk
    tri = jnp.asarray(np.tril(np.ones((chunk, chunk), np.float32)), BF16)
    lm = jnp.asarray(_hgrn_masks(chunk))
    blk = pl.BlockSpec((chunk, HGRN_WIDTH), lambda b, c: (b * nc + c, 0))
    full = lambda a: pl.BlockSpec(a.shape, lambda b, c: (0,) * a.ndim)
    return pl.pallas_call(
        functools.partial(_hgrn_kernel, chunk),
        grid=(batch, nc),
        in_specs=[blk, blk, blk, blk, blk, full(s0), full(ng), full(tri), full(lm)],
        out_specs=(blk, pl.BlockSpec((1, HGRN_HEADS, HGRN_EXPAND, HGRN_EXPAND),
                                     lambda b, c: (b, 0, 0, 0))),
        out_shape=(jax.ShapeDtypeStruct((batch * seq, HGRN_WIDTH), F32),
                   jax.ShapeDtypeStruct((batch, HGRN_HEADS, HGRN_EXPAND, HGRN_EXPAND), F32)),
        scratch_shapes=[pltpu.VMEM((HGRN_HEADS, HGRN_EXPAND, HGRN_EXPAND), F32)],
        compiler_params=_params("arbitrary", "arbitrary"),
        name=f"hgrn2_c{chunk}",
    )(qf, g, kk, v, gh, s0, ng, tri, lm)


def _out_kernel(a_ref, r_ref, h_ref, wo_ref, lg_ref, lb_ref, o_ref):
    y = (_dot(a_ref[...].astype(BF16), wo_ref[:ATTN_WIDTH, :])
         + _dot(r_ref[...].astype(BF16), wo_ref[ATTN_WIDTH:, :]))
    z = DN_ALPHA * h_ref[...] + y
    mu = jnp.mean(z, axis=-1, keepdims=True)
    zc = z - mu
    var = jnp.mean(zc * zc, axis=-1, keepdims=True)
    o_ref[...] = zc * lax.rsqrt(var + EPS) * lg_ref[...] + lb_ref[...]


def _out_project(a, r, h2d, wo, lng, lnb, rows):
    t = h2d.shape[0]
    row_spec = lambda w: pl.BlockSpec((rows, w), lambda i: (i, 0))
    full = lambda x: pl.BlockSpec(x.shape, lambda i: (0,) * x.ndim)
    return pl.pallas_call(
        _out_kernel,
        grid=(t // rows,),
        in_specs=[row_spec(ATTN_WIDTH), row_spec(HGRN_WIDTH), row_spec(D_MODEL),
                  full(wo), full(lng), full(lnb)],
        out_specs=row_spec(D_MODEL),
        out_shape=jax.ShapeDtypeStruct((t, D_MODEL), F32),
        compiler_params=_params("arbitrary"),
        name=f"out_proj_r{rows}",
    )(a, r, h2d, wo, lng, lnb)


def _block_diag(blocks):
    n, r, c = blocks.shape
    out = jnp.zeros((n * r, n * c), blocks.dtype)
    for i in range(n):
        out = out.at[i * r:(i + 1) * r, i * c:(i + 1) * c].set(blocks[i])
    return out


def _pad_heads(w_uv):
    n, r, c = w_uv.shape
    out = jnp.zeros((n, r, n * c), w_uv.dtype)
    for i in range(n):
        out = out.at[i, :, i * c:(i + 1) * c].set(w_uv[i])
    return out


def kernel(x, meta_tokens, rel_bias, hgrn_lb_raw, w_in, kv_norm_g, w_uk, w_uv, hgrn_norm_g, w_out, ln_g, ln_b):
    batch, seq, _ = x.shape
    assert seq % QB == 0 and seq % HGRN_CHUNK == 0 and (batch * seq) % PROJ_ROWS == 0
    k_top = min(TOPK_MAX, seq // 4)

    tbd, tbp, tm0 = _bias_tables(rel_bias.astype(F32))
    lbraw = hgrn_lb_raw.astype(F32)
    h = x.reshape(batch * seq, D_MODEL).astype(F32)
    hm = meta_tokens.astype(F32)
    s_zero = jnp.zeros((HGRN_HEADS, HGRN_EXPAND, HGRN_EXPAND), F32)

    for l in range(DEPTH):
        wp = jnp.concatenate(
            [w_in[l][:, :_RAW_SPLIT], jnp.zeros((D_MODEL, _P_GA - _RAW_SPLIT), w_in.dtype),
             w_in[l][:, _RAW_SPLIT:]], axis=1).astype(BF16)
        wuk_bd = _block_diag(w_uk[l]).astype(BF16)
        wuv_pad = _pad_heads(w_uv[l]).astype(BF16)
        wo = w_out[l].astype(BF16)
        kvg = kv_norm_g[l].reshape(1, KV_RANK).astype(F32)
        ng = hgrn_norm_g[l].reshape(1, HGRN_EXPAND).astype(F32)
        lng = ln_g[l].reshape(1, D_MODEL).astype(F32)
        lnb = ln_b[l].reshape(1, D_MODEL).astype(F32)

        (ql_m, cm, _, _, _, ga_m, qf_m, g_m, kk_m, v_m, gh_m) = _project(
            l, hm, wp, wuk_bd, kvg, lbraw, N_META)
        a_m = _meta_attention(ql_m, cm, ga_m, tbd, wuv_pad)
        r_m, s_m = _hgrn(1, N_META, N_META, qf_m, g_m, kk_m, v_m, gh_m, s_zero, ng)

        (ql, cb, qi, kib, kw, ga, qf, g, kk, v, gh) = _project(
            l, h, wp, wuk_bd, kvg, lbraw, PROJ_ROWS)
        a = _attention(k_top, batch, seq, qi, kib, kw, ql, cb, cm, ga, tbd, tbp, tm0,
                       rel_bias.astype(F32), wuv_pad)
        r, _ = _hgrn(batch, seq, HGRN_CHUNK, qf, g, kk, v, gh, s_m[0], ng)

        hm = _out_project(a_m, r_m, hm, wo, lng, lnb, N_META)
        h = _out_project(a, r, h, wo, lng, lnb, PROJ_ROWS)

    return h.reshape(batch, seq, D_MODEL).astype(x.dtype)
```

```python
import functools
import math

import numpy as np
import jax
import jax.numpy as jnp
from jax import lax
from jax.experimental import pallas as pl
from jax.experimental.pallas import tpu as pltpu

D_MODEL = 1024
DEPTH = 2
N_META = 16
ATTN_WIDTH = 512
HGRN_WIDTH = 512
ATTN_HEADS = 8
ATTN_HEAD_DIM = 64
KV_RANK = 128
IDX_HEADS = 4
IDX_DIM = 64
TOPK_MAX = 256
HGRN_EXPAND = 128
HGRN_HEADS = 4
REL_BUCKETS = 32
REL_MAX_DIST = 128
DN_ALPHA = (2 * DEPTH) ** 0.25
EPS = 1e-6

F32 = jnp.float32
BF16 = jnp.bfloat16

LANES = 128
SUBLANES = 8
QB = 128
KEY_TILES = 4
CK = KEY_TILES * QB
HGRN_CHUNK = 128
PROJ_ROWS = 256
VMEM_LIMIT = 56 * 1024 * 1024

_P_QA, _P_CKV, _P_QI, _P_KW, _P_GA, _P_QH, _P_FH, _P_IH, _P_GH, _P_END = (
    0, 512, 640, 896, 1024, 1536, 2048, 2560, 3072, 3584)
_RAW_SPLIT = 964

NEG_INF_KEY = -2139095041
INT_MIN = -2147483648


def _bucket_starts():
    max_exact = REL_BUCKETS // 2
    n = np.arange(0, 4 * REL_MAX_DIST)
    nf = np.maximum(n, 1).astype(np.float64)
    large = max_exact + (np.log(nf / max_exact) / math.log(REL_MAX_DIST / max_exact)
                         * (REL_BUCKETS - max_exact)).astype(np.int64)
    large = np.minimum(large, REL_BUCKETS - 1)
    bucket = np.where(n < max_exact, n, large)
    return [int(np.argmax(bucket >= b)) for b in range(REL_BUCKETS)]


_BUCKET_START = _bucket_starts()


def _dot(a, b):
    return jnp.dot(a, b, preferred_element_type=F32)


def _dot_nt(a, b):
    return lax.dot_general(a, b, (((1,), (1,)), ((), ())), preferred_element_type=F32)


def _dot_tn(a, b):
    return lax.dot_general(a, b, (((0,), (0,)), ((), ())), preferred_element_type=F32)


def _sigmoid(x):
    return 1.0 / (1.0 + jnp.exp(-x))


def _params(*sem):
    return pltpu.CompilerParams(dimension_semantics=sem, vmem_limit_bytes=VMEM_LIMIT)


def _fold_rows(x, op):
    parts = [x[r:r + SUBLANES] for r in range(0, x.shape[0], SUBLANES)]
    while len(parts) > 1:
        parts = [op(parts[k], parts[k + 1]) for k in range(0, len(parts), 2)]
    return parts[0]


def _proj_kernel(layer, h_ref, wp_ref, wuk_ref, kvg_ref, lbraw_ref,
                 qlt_ref, cb_ref, cbt_ref, qit_ref, kib_ref, wt_ref, ga_ref,
                 qf_ref, g_ref, kk_ref, v_ref, gh_ref):
    rows = h_ref.shape[0]
    hb = h_ref[...].astype(BF16)

    def proj(lo, hi):
        return _dot(hb, wp_ref[:, lo:hi])

    ql = _dot(proj(_P_QA, _P_CKV).astype(BF16), wuk_ref[...]) * (ATTN_HEAD_DIM ** -0.5)
    qlt_ref[...] = ql.T.reshape(ATTN_HEADS, KV_RANK, rows).astype(BF16)

    ckv = proj(_P_CKV, _P_QI)
    c = ckv * lax.rsqrt(jnp.mean(ckv * ckv, axis=-1, keepdims=True) + EPS) * kvg_ref[...]
    cb_ref[...] = c.astype(BF16)
    ct = c.T
    for t in range(rows // QB):
        cbt_ref[t] = ct[:, t * QB:(t + 1) * QB].astype(BF16)

    qit_ref[...] = proj(_P_QI, _P_KW).T.astype(BF16)
    kw = proj(_P_KW, _P_GA)
    kib_ref[...] = kw[:, :IDX_DIM].astype(BF16)
    wt_ref[...] = kw.T[IDX_DIM:IDX_DIM + SUBLANES, :] * (IDX_HEADS ** -0.5 * IDX_DIM ** -0.5)

    ga = proj(_P_GA, _P_QH)
    ga_ref[...] = ga * _sigmoid(ga)

    qh = proj(_P_QH, _P_FH)
    qf_ref[...] = qh * _sigmoid(qh) * (HGRN_EXPAND ** -0.5)

    raw = lbraw_ref[...]
    ex = jnp.exp(raw - jnp.max(raw, axis=0, keepdims=True))
    lbp = ex / jnp.sum(ex, axis=0, keepdims=True)
    lb = lbp[0:1]
    for i in range(1, layer + 1):
        lb = lb + lbp[i:i + 1]
    lb = lb - lbp[0:1]
    f = lb + (1.0 - lb) * _sigmoid(proj(_P_FH, _P_IH))
    g_ref[...] = jnp.log(f)
    kk_ref[...] = 1.0 - f
    v_ref[...] = proj(_P_IH, _P_GH)
    gh = proj(_P_GH, _P_END)
    gh_ref[...] = gh * _sigmoid(gh)


def _project(layer, h2d, wp, wuk_bd, kvg, lbraw, rows):
    t = h2d.shape[0]
    grid = (t // rows,)
    row_spec = lambda w: pl.BlockSpec((rows, w), lambda r: (r, 0))
    col_spec = lambda n: pl.BlockSpec((n, rows), lambda r: (0, r))
    full = lambda a: pl.BlockSpec(a.shape, lambda r: (0,) * a.ndim)
    out_shape = (
        jax.ShapeDtypeStruct((ATTN_HEADS, KV_RANK, t), BF16),
        jax.ShapeDtypeStruct((t, KV_RANK), BF16),
        jax.ShapeDtypeStruct((t // QB, KV_RANK, QB), BF16),
        jax.ShapeDtypeStruct((IDX_HEADS * IDX_DIM, t), BF16),
        jax.ShapeDtypeStruct((t, IDX_DIM), BF16),
        jax.ShapeDtypeStruct((SUBLANES, t), F32),
        jax.ShapeDtypeStruct((t, ATTN_WIDTH), F32),
        jax.ShapeDtypeStruct((t, HGRN_WIDTH), F32),
        jax.ShapeDtypeStruct((t, HGRN_WIDTH), F32),
        jax.ShapeDtypeStruct((t, HGRN_WIDTH), F32),
        jax.ShapeDtypeStruct((t, HGRN_WIDTH), F32),
        jax.ShapeDtypeStruct((t, HGRN_WIDTH), F32),
    )
    out_specs = (
        pl.BlockSpec((ATTN_HEADS, KV_RANK, rows), lambda r: (0, 0, r)),
        row_spec(KV_RANK),
        pl.BlockSpec((rows // QB, KV_RANK, QB), lambda r: (r, 0, 0)),
        col_spec(IDX_HEADS * IDX_DIM), row_spec(IDX_DIM), col_spec(SUBLANES),
        row_spec(ATTN_WIDTH), row_spec(HGRN_WIDTH), row_spec(HGRN_WIDTH), row_spec(HGRN_WIDTH),
        row_spec(HGRN_WIDTH), row_spec(HGRN_WIDTH),
    )
    return pl.pallas_call(
        functools.partial(_proj_kernel, layer),
        grid=grid,
        in_specs=[row_spec(D_MODEL), full(wp), full(wuk_bd), full(kvg), full(lbraw)],
        out_specs=out_specs,
        out_shape=out_shape,
        compiler_params=_params("arbitrary"),
        name=f"proj_l{layer}_r{rows}",
    )(h2d, wp, wuk_bd, kvg, lbraw)


def _bias_tile(dist, rb_ref, h):
    out = jnp.full(dist.shape, rb_ref[0, h], F32)
    for b in range(1, REL_BUCKETS):
        out = jnp.where(dist >= _BUCKET_START[b], rb_ref[b, h], out)
    return out


def _bias_kernel(rb_ref, tbd_ref, tbp_ref, tm0_ref, bfar_ref):
    k = lax.broadcasted_iota(jnp.int32, (QB, QB), 0)
    q = lax.broadcasted_iota(jnp.int32, (QB, QB), 1)
    km = lax.broadcasted_iota(jnp.int32, (N_META, QB), 0)
    qm = lax.broadcasted_iota(jnp.int32, (N_META, QB), 1)
    for h in range(ATTN_HEADS):
        lanes = slice(h * QB, (h + 1) * QB)
        tbd_ref[:, lanes] = _bias_tile(jnp.maximum(q - k, 0), rb_ref, h)
        tbp_ref[:, lanes] = _bias_tile(q - k + QB, rb_ref, h)
        tm0_ref[:, lanes] = _bias_tile(N_META + qm - km, rb_ref, h)
        bfar_ref[:, lanes] = jnp.full((SUBLANES, QB), rb_ref[REL_BUCKETS - 1, h], F32)


def _bias_tables(rel_bias):
    wide = ATTN_HEADS * QB
    return pl.pallas_call(
        _bias_kernel,
        in_specs=[pl.BlockSpec(memory_space=pltpu.SMEM)],
        out_shape=(jax.ShapeDtypeStruct((QB, wide), F32), jax.ShapeDtypeStruct((QB, wide), F32),
                   jax.ShapeDtypeStruct((N_META, wide), F32), jax.ShapeDtypeStruct((SUBLANES, wide), F32)),
        name="bias_tables",
    )(rel_bias)


def _to_key(s):
    bits = pltpu.bitcast(s, jnp.int32)
    return jnp.where(bits < 0, bits ^ 0x7FFFFFFF, bits)


def _attn_kernel(k_top, qit_ref, kib_ref, wt_ref, qlt_ref, cb_ref, cbt_ref, cm_ref, cmt_ref, ga_ref,
                 tbd_ref, tbp_ref, tm0_ref, bfar_ref, wuvt_ref, a_ref,
                 key_scr, am_scr, lg_scr, oacc_scr):
    i = pl.program_id(1)
    nch = i // KEY_TILES + 1
    wide = ATTN_HEADS * QB

    def rows_of(c):
        return pl.ds(pl.multiple_of(c * CK, CK), CK)

    def tile_rows(j):
        return pl.ds(pl.multiple_of(j * QB, QB), QB)

    qit = qit_ref[...]
    wt = wt_ref[...]
    k_loc = lax.broadcasted_iota(jnp.int32, (CK, QB), 0)
    q_pos = i * QB + lax.broadcasted_iota(jnp.int32, (CK, QB), 1)

    def score_body(c, carry):
        kc = kib_ref[rows_of(c), :]
        s = jnp.maximum(_dot(kc, qit[0:IDX_DIM]), 0.0) * wt[0:1]
        for h in range(1, IDX_HEADS):
            s = s + jnp.maximum(_dot(kc, qit[h * IDX_DIM:(h + 1) * IDX_DIM]), 0.0) * wt[h:h + 1]
        s = jnp.where(c * CK + k_loc <= q_pos, s, -jnp.inf)
        key_scr[rows_of(c), :] = _to_key(s)
        return carry

    lax.fori_loop(0, nch, score_body, 0)

    def count(pred):
        def body(c, acc):
            hit = jnp.where(pred(key_scr[rows_of(c), :]), 1.0, 0.0)
            return acc + jnp.sum(hit.reshape(KEY_TILES, QB, QB), axis=0)
        acc = lax.fori_loop(0, nch, body, jnp.zeros((QB, QB), F32))
        return jnp.sum(acc, axis=0, keepdims=True)

    kf = float(k_top)
    zero = jnp.zeros((1, QB), jnp.int32)
    prefix = jnp.where(count(lambda key: key >= zero) >= kf, zero, jnp.full((1, QB), INT_MIN, jnp.int32))

    def bit_body(t, prefix):
        cand = prefix | jnp.left_shift(jnp.int32(1), 30 - t)
        return jnp.where(count(lambda key: key >= cand) >= kf, cand, prefix)

    thr = lax.fori_loop(0, 31, bit_body, prefix)

    n_gt = count(lambda key: key > thr)
    need = jnp.where(thr > NEG_INF_KEY, kf - n_gt, 0.0)
    thr_valid = jnp.maximum(thr, NEG_INF_KEY)
    tri = jnp.where(lax.broadcasted_iota(jnp.int32, (QB, QB), 1)
                    <= lax.broadcasted_iota(jnp.int32, (QB, QB), 0), 1.0, 0.0).astype(BF16)

    def sel_body(c, run):
        for t in range(KEY_TILES):
            tile = tile_rows(c * KEY_TILES + t)
            key = key_scr[tile, :]
            eq = key == thr
            pre = _dot(tri, jnp.where(eq, 1.0, 0.0).astype(BF16))
            tie = jnp.where(eq, jnp.where(run + pre <= need, 0.0, -jnp.inf), -jnp.inf)
            am_scr[tile, :] = jnp.where(key > thr_valid, 0.0, tie)
            run = run + pre[QB - 1:QB, :]
        return run

    lax.fori_loop(0, nch, sel_body, jnp.zeros((1, QB), F32))

    qt = jnp.concatenate([qlt_ref[h] for h in range(ATTN_HEADS)], axis=1)
    bfar = bfar_ref[0:1, :]

    def masked(lg, rows):
        am = am_scr[rows, :]
        return lg + jnp.concatenate([am] * ATTN_HEADS, axis=1)

    def logit_body(c, m):
        lg = masked(_dot(cb_ref[rows_of(c), :], qt) + bfar, rows_of(c))
        lg_scr[rows_of(c), :] = lg
        for t in range(KEY_TILES):
            j = c * KEY_TILES + t
            mt = _fold_rows(lg[t * QB:(t + 1) * QB], jnp.maximum)
            m = jnp.maximum(m, jnp.where((j == i) | (j == i - 1), -jnp.inf, mt))
        return m

    m = lax.fori_loop(0, nch, logit_body, jnp.full((SUBLANES, wide), -jnp.inf, F32))

    jp = jnp.maximum(i - 1, 0)
    lgp = masked(_dot(cb_ref[tile_rows(jp), :], qt) + tbp_ref[...], tile_rows(jp))
    lgp = lgp + jnp.where(i >= 1, 0.0, -jnp.inf)
    lg_scr[tile_rows(jp), :] = lgp
    lgd = masked(_dot(cb_ref[tile_rows(i), :], qt) + tbd_ref[...], tile_rows(i))
    lg_scr[tile_rows(i), :] = lgd
    lgm = _dot(cm_ref[...], qt) + jnp.where(i == 0, tm0_ref[...], bfar)
    m = jnp.maximum(m, jnp.maximum(_fold_rows(lgp, jnp.maximum), _fold_rows(lgd, jnp.maximum)))
    m_row = jnp.maximum(jnp.max(m, axis=0, keepdims=True), jnp.max(lgm, axis=0, keepdims=True))

    oacc_scr[...] = jnp.zeros_like(oacc_scr)

    def pv_body(c, l):
        pr = jnp.exp(lg_scr[rows_of(c), :] - m_row)
        ct = cbt_ref[pl.ds(c * KEY_TILES, KEY_TILES)]
        ct = jnp.concatenate([ct[t] for t in range(KEY_TILES)], axis=1)
        oacc_scr[...] += _dot(ct, pr.astype(BF16))
        return l + _fold_rows(pr, jnp.add)

    l = lax.fori_loop(0, nch, pv_body, jnp.zeros((SUBLANES, wide), F32))
    pm = jnp.exp(lgm - m_row)
    l_row = jnp.sum(l, axis=0, keepdims=True) + jnp.sum(pm, axis=0, keepdims=True)
    o = ((oacc_scr[...] + _dot(cmt_ref[...], pm.astype(BF16))) / l_row).astype(BF16)
    o_all = jnp.concatenate([o[:, h * QB:(h + 1) * QB] for h in range(ATTN_HEADS)], axis=0)
    a_t = _dot(wuvt_ref[...], o_all)
    a_ref[...] = a_t.T * ga_ref[...]


def _attention(k_top, batch, seq, qit, kib, wt, qlt, cb, cbt, cm, cmt, ga, tbd, tbp, tm0, bfar, wuvt):
    nq = seq // QB
    full = lambda a: pl.BlockSpec(a.shape, lambda b, i: (0,) * a.ndim)
    qcols = lambda n: pl.BlockSpec((n, QB), lambda b, i: (0, b * nq + i))
    per_batch = lambda w: pl.BlockSpec((seq, w), lambda b, i: (b, 0))
    return pl.pallas_call(
        functools.partial(_attn_kernel, k_top),
        grid=(batch, nq),
        in_specs=[
            qcols(IDX_HEADS * IDX_DIM),
            per_batch(IDX_DIM),
            qcols(SUBLANES),
            pl.BlockSpec((ATTN_HEADS, KV_RANK, QB), lambda b, i: (0, 0, b * nq + i)),
            per_batch(KV_RANK),
            pl.BlockSpec((nq, KV_RANK, QB), lambda b, i: (b, 0, 0)),
            full(cm), full(cmt),
            pl.BlockSpec((QB, ATTN_WIDTH), lambda b, i: (b * nq + i, 0)),
            full(tbd), full(tbp), full(tm0), full(bfar), full(wuvt),
        ],
        out_specs=pl.BlockSpec((QB, ATTN_WIDTH), lambda b, i: (b * nq + i, 0)),
        out_shape=jax.ShapeDtypeStruct((batch * seq, ATTN_WIDTH), F32),
        scratch_shapes=[
            pltpu.VMEM((seq, QB), jnp.int32),
            pltpu.VMEM((seq, QB), F32),
            pltpu.VMEM((seq, ATTN_HEADS * QB), F32),
            pltpu.VMEM((KV_RANK, ATTN_HEADS * QB), F32),
        ],
        compiler_params=_params("arbitrary", "arbitrary"),
        name="dsa_attention",
    )(qit, kib, wt, qlt, cb, cbt, cm, cmt, ga, tbd, tbp, tm0, bfar, wuvt)


def _meta_attn_kernel(qlt_ref, cm_ref, cmt_ref, ga_ref, tbd_ref, wuvt_ref, a_ref, o_scr):
    cm = cm_ref[...]
    cmt = cmt_ref[...]
    k = lax.broadcasted_iota(jnp.int32, (N_META, N_META), 0)
    q = lax.broadcasted_iota(jnp.int32, (N_META, N_META), 1)
    for h in range(ATTN_HEADS):
        lg = _dot(cm, qlt_ref[h]) + tbd_ref[:N_META, h * QB:h * QB + N_META]
        lg = jnp.where(k <= q, lg, -jnp.inf)
        pr = jnp.exp(lg - jnp.max(lg, axis=0, keepdims=True))
        pr = pr / jnp.sum(pr, axis=0, keepdims=True)
        o_scr[h * KV_RANK:(h + 1) * KV_RANK, :] = _dot(cmt, pr.astype(BF16)).astype(BF16)
    a_ref[...] = _dot(wuvt_ref[...], o_scr[...]).T * ga_ref[...]


def _meta_attention(qlt_m, cm, cmt, ga_m, tbd, wuvt):
    return pl.pallas_call(
        _meta_attn_kernel,
        out_shape=jax.ShapeDtypeStruct((N_META, ATTN_WIDTH), F32),
        scratch_shapes=[pltpu.VMEM((ATTN_HEADS * KV_RANK, N_META), BF16)],
        name="meta_attention",
    )(qlt_m, cm, cmt, ga_m, tbd, wuvt)


def _hgrn_levels(chunk):
    return [1 << p for p in range(int(math.log2(chunk)))]


def _hgrn_masks(chunk):
    t = np.arange(chunk)[:, None]
    s = np.arange(chunk)[None, :]
    masks = [t == s]
    for m in _hgrn_levels(chunk):
        same = (t // (2 * m)) == (s // (2 * m))
        masks.append(same & ((t & m) != 0) & ((s & m) == 0))
    return np.stack(masks).astype(np.float32)


def _block_ref_rows(b, m, chunk, row):
    if 2 * m >= SUBLANES:
        pieces = []
        for blk in range(chunk // (2 * m)):
            r = blk * 2 * m + m - 1
            pieces.append(jnp.broadcast_to(b[r:r + 1, :], (2 * m, b.shape[1])))
        return pieces[0] if len(pieces) == 1 else jnp.concatenate(pieces, axis=0)
    delta = (row & (2 * m - 1)) - (m - 1)
    out = b
    for d in range(-(m - 1), m + 1):
        if d != 0:
            out = jnp.where(delta == d, pltpu.roll(b, d % chunk, 0), out)
    return out


def _split3(x):
    hi = x.astype(BF16)
    r1 = x - hi.astype(F32)
    mid = r1.astype(BF16)
    lo = (r1 - mid.astype(F32)).astype(BF16)
    return hi, mid, lo


def _hgrn_kernel(chunk, qf_ref, g_ref, kk_ref, v_ref, gh_ref, s0_ref, ng_ref, tri_ref, lm_ref,
                 r_ref, sfin_ref, st_scr):
    ci = pl.program_id(1)

    @pl.when(ci == 0)
    def _():
        st_scr[...] = s0_ref[...]

    tri = tri_ref[...]
    row = lax.broadcasted_iota(jnp.int32, (chunk, HGRN_EXPAND), 0)
    levels = _hgrn_levels(chunk)
    for hd in range(HGRN_HEADS):
        sl = slice(hd * HGRN_EXPAND, (hd + 1) * HGRN_EXPAND)
        q = qf_ref[:, sl]
        g = g_ref[:, sl]
        k = kk_ref[:, sl]
        vb = v_ref[:, sl].astype(BF16)
        g_hi, g_mid, g_lo = _split3(g)
        b = _dot(tri, g_hi) + _dot(tri, g_mid) + _dot(tri, g_lo)

        a = _dot_nt(q.astype(BF16), k.astype(BF16)) * lm_ref[0]
        for li, m in enumerate(levels):
            bref = _block_ref_rows(b, m, chunk, row)
            e = jnp.exp(jnp.where((row & m) != 0, b - bref, bref - b))
            a = a + _dot_nt((q * e).astype(BF16), (k * e).astype(BF16)) * lm_ref[li + 1]

        st = st_scr[hd]
        o = _dot_nt((q * jnp.exp(b)).astype(BF16), st.astype(BF16)) + _dot(a.astype(BF16), vb)
        b_last = b[chunk - 1:chunk, :]
        kd = (k * jnp.exp(b_last - b)).astype(BF16)
        st_scr[hd] = st * jnp.exp(b_last) + _dot_tn(vb, kd)

        rn = o * lax.rsqrt(jnp.mean(o * o, axis=-1, keepdims=True) + EPS) * ng_ref[...]
        r_ref[:, sl] = rn * gh_ref[:, sl]

    @pl.when(ci == pl.num_programs(1) - 1)
    def _():
        sfin_ref[0] = st_scr[...]


def _hgrn(batch, seq, chunk, qf, g, kk, v, gh, s0, ng):
    nc = seq // chunk
    tri = jnp.asarray(np.tril(np.ones((chunk, chunk), np.float32)), BF16)
    lm = jnp.asarray(_hgrn_masks(chunk))
    blk = pl.BlockSpec((chunk, HGRN_WIDTH), lambda b, c: (b * nc + c, 0))
    full = lambda a: pl.BlockSpec(a.shape, lambda b, c: (0,) * a.ndim)
    return pl.pallas_call(
        functools.partial(_hgrn_kernel, chunk),
        grid=(batch, nc),
        in_specs=[blk, blk, blk, blk, blk, full(s0), full(ng), full(tri), full(lm)],
        out_specs=(blk, pl.BlockSpec((1, HGRN_HEADS, HGRN_EXPAND, HGRN_EXPAND),
                                     lambda b, c: (b, 0, 0, 0))),
        out_shape=(jax.ShapeDtypeStruct((batch * seq, HGRN_WIDTH), F32),
                   jax.ShapeDtypeStruct((batch, HGRN_HEADS, HGRN_EXPAND, HGRN_EXPAND), F32)),
        scratch_shapes=[pltpu.VMEM((HGRN_HEADS, HGRN_EXPAND, HGRN_EXPAND), F32)],
        compiler_params=_params("arbitrary", "arbitrary"),
        name=f"hgrn2_c{chunk}",
    )(qf, g, kk, v, gh, s0, ng, tri, lm)


def _out_kernel(a_ref, r_ref, h_ref, wo_ref, lg_ref, lb_ref, o_ref):
    y = (_dot(a_ref[...].astype(BF16), wo_ref[:ATTN_WIDTH, :])
         + _dot(r_ref[...].astype(BF16), wo_ref[ATTN_WIDTH:, :]))
    z = DN_ALPHA * h_ref[...] + y
    mu = jnp.mean(z, axis=-1, keepdims=True)
    zc = z - mu
    var = jnp.mean(zc * zc, axis=-1, keepdims=True)
    o_ref[...] = zc * lax.rsqrt(var + EPS) * lg_ref[...] + lb_ref[...]


def _out_project(a, r, h2d, wo, lng, lnb, rows):
    t = h2d.shape[0]
    row_spec = lambda w: pl.BlockSpec((rows, w), lambda i: (i, 0))
    full = lambda x: pl.BlockSpec(x.shape, lambda i: (0,) * x.ndim)
    return pl.pallas_call(
        _out_kernel,
        grid=(t // rows,),
        in_specs=[row_spec(ATTN_WIDTH), row_spec(HGRN_WIDTH), row_spec(D_MODEL),
                  full(wo), full(lng), full(lnb)],
        out_specs=row_spec(D_MODEL),
        out_shape=jax.ShapeDtypeStruct((t, D_MODEL), F32),
        compiler_params=_params("arbitrary"),
        name=f"out_proj_r{rows}",
    )(a, r, h2d, wo, lng, lnb)


def _block_diag(blocks):
    n, r, c = blocks.shape
    eye = jnp.eye(n, dtype=blocks.dtype)
    return (eye[:, None, :, None] * blocks[:, :, None, :]).reshape(n * r, n * c)


def kernel(x, meta_tokens, rel_bias, hgrn_lb_raw, w_in, kv_norm_g, w_uk, w_uv, hgrn_norm_g, w_out, ln_g, ln_b):
    batch, seq, _ = x.shape
    assert seq % CK == 0 and seq % HGRN_CHUNK == 0 and (batch * seq) % PROJ_ROWS == 0
    k_top = min(TOPK_MAX, seq // 4)

    rel_bias = rel_bias.astype(F32)
    tbd, tbp, tm0, bfar = _bias_tables(rel_bias)
    lbraw = hgrn_lb_raw.astype(F32)
    h = x.reshape(batch * seq, D_MODEL).astype(F32)
    hm = meta_tokens.astype(F32)
    s_zero = jnp.zeros((HGRN_HEADS, HGRN_EXPAND, HGRN_EXPAND), F32)

    for l in range(DEPTH):
        wp = jnp.concatenate(
            [w_in[l][:, :_RAW_SPLIT], jnp.zeros((D_MODEL, _P_GA - _RAW_SPLIT), w_in.dtype),
             w_in[l][:, _RAW_SPLIT:]], axis=1).astype(BF16)
        wuk_bd = _block_diag(w_uk[l]).astype(BF16)
        wuvt = _block_diag(w_uv[l]).T.astype(BF16)
        wo = w_out[l].astype(BF16)
        kvg = kv_norm_g[l].reshape(1, KV_RANK).astype(F32)
        ng = hgrn_norm_g[l].reshape(1, HGRN_EXPAND).astype(F32)
        lng = ln_g[l].reshape(1, D_MODEL).astype(F32)
        lnb = ln_b[l].reshape(1, D_MODEL).astype(F32)

        hm_pad = jnp.pad(hm, ((0, QB - N_META), (0, 0)))
        (qlt_m, cb_m, cbt_m, _, _, _, ga_m, qf_m, g_m, kk_m, v_m, gh_m) = _project(
            l, hm_pad, wp, wuk_bd, kvg, lbraw, QB)
        cm = cb_m[:N_META]
        cmt = cbt_m[0][:, :N_META]
        a_m = _meta_attention(qlt_m[:, :, :N_META], cm, cmt, ga_m[:N_META], tbd, wuvt)
        r_m, s_m = _hgrn(1, N_META, N_META, qf_m[:N_META], g_m[:N_META], kk_m[:N_META],
                         v_m[:N_META], gh_m[:N_META], s_zero, ng)

        (qlt, cb, cbt, qit, kib, wt, ga, qf, g, kk, v, gh) = _project(
            l, h, wp, wuk_bd, kvg, lbraw, PROJ_ROWS)
        a = _attention(k_top, batch, seq, qit, kib, wt, qlt, cb, cbt, cm, cmt, ga,
                       tbd, tbp, tm0, bfar, wuvt)
        r, _ = _hgrn(batch, seq, HGRN_CHUNK, qf, g, kk, v, gh, s_m[0], ng)

        hm = _out_project(a_m, r_m, hm, wo, lng, lnb, N_META)
        h = _out_project(a, r, h, wo, lng, lnb, PROJ_ROWS)

    return h.reshape(batch, seq, D_MODEL).astype(x.dtype)
```

```python
import functools
import math

import numpy as np
import jax
import jax.numpy as jnp
from jax import lax
from jax.experimental import pallas as pl
from jax.experimental.pallas import tpu as pltpu

D_MODEL = 1024
DEPTH = 2
N_META = 16
ATTN_WIDTH = 512
HGRN_WIDTH = 512
ATTN_HEADS = 8
ATTN_HEAD_DIM = 64
KV_RANK = 128
IDX_HEADS = 4
IDX_DIM = 64
TOPK_MAX = 256
HGRN_EXPAND = 128
HGRN_HEADS = 4
REL_BUCKETS = 32
REL_MAX_DIST = 128
DN_ALPHA = (2 * DEPTH) ** 0.25
EPS = 1e-6

F32 = jnp.float32
BF16 = jnp.bfloat16

LANES = 128
SUBLANES = 8
QB = 128
KEY_TILES = 4
CK = KEY_TILES * QB
WORD_BITS = 32
GROUP = WORD_BITS * SUBLANES
HGRN_CHUNK = 128
PROJ_ROWS = 256
VMEM_LIMIT = 56 * 1024 * 1024

_P_QA, _P_CKV, _P_QI, _P_KW, _P_GA, _P_QH, _P_FH, _P_IH, _P_GH, _P_END = (
    0, 512, 640, 896, 1024, 1536, 2048, 2560, 3072, 3584)
_RAW_SPLIT = 964

NEG_INF_KEY = -2139095041
INT_MIN = -2147483648


def _bucket_starts():
    max_exact = REL_BUCKETS // 2
    n = np.arange(0, 4 * REL_MAX_DIST)
    nf = np.maximum(n, 1).astype(np.float64)
    large = max_exact + (np.log(nf / max_exact) / math.log(REL_MAX_DIST / max_exact)
                         * (REL_BUCKETS - max_exact)).astype(np.int64)
    large = np.minimum(large, REL_BUCKETS - 1)
    bucket = np.where(n < max_exact, n, large)
    return [int(np.argmax(bucket >= b)) for b in range(REL_BUCKETS)]


_BUCKET_START = _bucket_starts()


def _dot(a, b):
    return jnp.dot(a, b, preferred_element_type=F32)


def _dot_nt(a, b):
    return lax.dot_general(a, b, (((1,), (1,)), ((), ())), preferred_element_type=F32)


def _dot_tn(a, b):
    return lax.dot_general(a, b, (((0,), (0,)), ((), ())), preferred_element_type=F32)


def _sigmoid(x):
    return 1.0 / (1.0 + jnp.exp(-x))


def _params(*sem):
    return pltpu.CompilerParams(dimension_semantics=sem, vmem_limit_bytes=VMEM_LIMIT)


def _fold_rows(x, op):
    parts = [x[r:r + SUBLANES] for r in range(0, x.shape[0], SUBLANES)]
    while len(parts) > 1:
        parts = [op(parts[k], parts[k + 1]) for k in range(0, len(parts), 2)]
    return parts[0]


def _proj_kernel(layer, h_ref, wp_ref, wuk_ref, kvg_ref, lbraw_ref,
                 qlt_ref, cb_ref, cbt_ref, qit_ref, kib_ref, wt_ref, ga_ref,
                 qf_ref, g_ref, kk_ref, v_ref, gh_ref):
    rows = h_ref.shape[0]
    hb = h_ref[...].astype(BF16)

    def proj(lo, hi):
        return _dot(hb, wp_ref[:, lo:hi])

    ql = _dot(proj(_P_QA, _P_CKV).astype(BF16), wuk_ref[...]) * (ATTN_HEAD_DIM ** -0.5)
    qlt_ref[...] = ql.T.reshape(ATTN_HEADS, KV_RANK, rows).astype(BF16)

    ckv = proj(_P_CKV, _P_QI)
    c = ckv * lax.rsqrt(jnp.mean(ckv * ckv, axis=-1, keepdims=True) + EPS) * kvg_ref[...]
    cb_ref[...] = c.astype(BF16)
    ct = c.T
    for t in range(rows // QB):
        cbt_ref[t] = ct[:, t * QB:(t + 1) * QB].astype(BF16)

    qit_ref[...] = proj(_P_QI, _P_KW).T.astype(BF16)
    kw = proj(_P_KW, _P_GA)
    kib_ref[...] = kw[:, :IDX_DIM].astype(BF16)
    wt_ref[...] = kw.T[IDX_DIM:IDX_DIM + SUBLANES, :] * (IDX_HEADS ** -0.5 * IDX_DIM ** -0.5)

    ga = proj(_P_GA, _P_QH)
    ga_ref[...] = ga * _sigmoid(ga)

    qh = proj(_P_QH, _P_FH)
    qf_ref[...] = qh * _sigmoid(qh) * (HGRN_EXPAND ** -0.5)

    raw = lbraw_ref[...]
    ex = jnp.exp(raw - jnp.max(raw, axis=0, keepdims=True))
    lbp = ex / jnp.sum(ex, axis=0, keepdims=True)
    lb = lbp[0:1]
    for i in range(1, layer + 1):
        lb = lb + lbp[i:i + 1]
    lb = lb - lbp[0:1]
    f = lb + (1.0 - lb) * _sigmoid(proj(_P_FH, _P_IH))
    g_ref[...] = jnp.log(f)
    kk_ref[...] = 1.0 - f
    v_ref[...] = proj(_P_IH, _P_GH)
    gh = proj(_P_GH, _P_END)
    gh_ref[...] = gh * _sigmoid(gh)


def _project(layer, h2d, wp, wuk_bd, kvg, lbraw, rows):
    t = h2d.shape[0]
    grid = (t // rows,)
    row_spec = lambda w: pl.BlockSpec((rows, w), lambda r: (r, 0))
    col_spec = lambda n: pl.BlockSpec((n, rows), lambda r: (0, r))
    full = lambda a: pl.BlockSpec(a.shape, lambda r: (0,) * a.ndim)
    out_shape = (
        jax.ShapeDtypeStruct((ATTN_HEADS, KV_RANK, t), BF16),
        jax.ShapeDtypeStruct((t, KV_RANK), BF16),
        jax.ShapeDtypeStruct((t // QB, KV_RANK, QB), BF16),
        jax.ShapeDtypeStruct((IDX_HEADS * IDX_DIM, t), BF16),
        jax.ShapeDtypeStruct((t, IDX_DIM), BF16),
        jax.ShapeDtypeStruct((SUBLANES, t), F32),
        jax.ShapeDtypeStruct((t, ATTN_WIDTH), F32),
        jax.ShapeDtypeStruct((t, HGRN_WIDTH), F32),
        jax.ShapeDtypeStruct((t, HGRN_WIDTH), F32),
        jax.ShapeDtypeStruct((t, HGRN_WIDTH), F32),
        jax.ShapeDtypeStruct((t, HGRN_WIDTH), F32),
        jax.ShapeDtypeStruct((t, HGRN_WIDTH), F32),
    )
    out_specs = (
        pl.BlockSpec((ATTN_HEADS, KV_RANK, rows), lambda r: (0, 0, r)),
        row_spec(KV_RANK),
        pl.BlockSpec((rows // QB, KV_RANK, QB), lambda r: (r, 0, 0)),
        col_spec(IDX_HEADS * IDX_DIM), row_spec(IDX_DIM), col_spec(SUBLANES),
        row_spec(ATTN_WIDTH), row_spec(HGRN_WIDTH), row_spec(HGRN_WIDTH), row_spec(HGRN_WIDTH),
        row_spec(HGRN_WIDTH), row_spec(HGRN_WIDTH),
    )
    return pl.pallas_call(
        functools.partial(_proj_kernel, layer),
        grid=grid,
        in_specs=[row_spec(D_MODEL), full(wp), full(wuk_bd), full(kvg), full(lbraw)],
        out_specs=out_specs,
        out_shape=out_shape,
        compiler_params=_params("arbitrary"),
        name=f"proj_l{layer}_r{rows}",
    )(h2d, wp, wuk_bd, kvg, lbraw)


def _bias_tile(dist, rb_ref, h):
    out = jnp.full(dist.shape, rb_ref[0, h], F32)
    for b in range(1, REL_BUCKETS):
        out = jnp.where(dist >= _BUCKET_START[b], rb_ref[b, h], out)
    return out


def _bias_kernel(rb_ref, tbd_ref, tbp_ref, tm0_ref, bfar_ref):
    k = lax.broadcasted_iota(jnp.int32, (QB, QB), 0)
    q = lax.broadcasted_iota(jnp.int32, (QB, QB), 1)
    km = lax.broadcasted_iota(jnp.int32, (N_META, QB), 0)
    qm = lax.broadcasted_iota(jnp.int32, (N_META, QB), 1)
    for h in range(ATTN_HEADS):
        lanes = slice(h * QB, (h + 1) * QB)
        tbd_ref[:, lanes] = _bias_tile(jnp.maximum(q - k, 0), rb_ref, h)
        tbp_ref[:, lanes] = _bias_tile(q - k + QB, rb_ref, h)
        tm0_ref[:, lanes] = _bias_tile(N_META + qm - km, rb_ref, h)
        bfar_ref[:, lanes] = jnp.full((SUBLANES, QB), rb_ref[REL_BUCKETS - 1, h], F32)


def _bias_tables(rel_bias):
    wide = ATTN_HEADS * QB
    return pl.pallas_call(
        _bias_kernel,
        in_specs=[pl.BlockSpec(memory_space=pltpu.SMEM)],
        out_shape=(jax.ShapeDtypeStruct((QB, wide), F32), jax.ShapeDtypeStruct((QB, wide), F32),
                   jax.ShapeDtypeStruct((N_META, wide), F32), jax.ShapeDtypeStruct((SUBLANES, wide), F32)),
        name="bias_tables",
    )(rel_bias)


def _to_ukey(s):
    bits = pltpu.bitcast(s, jnp.int32)
    return jnp.where(bits < 0, ~bits, bits | INT_MIN)


def _bit_transpose(x):
    x = list(x)
    j, m = 16, 0x0000FFFF
    while j:
        k = 0
        while k < WORD_BITS:
            t = (x[k] ^ lax.shift_right_logical(x[k + j], jnp.int32(j))) & m
            x[k] = x[k] ^ t
            x[k + j] = x[k + j] ^ jnp.left_shift(t, j)
            k = (k + j + 1) & ~j
        j >>= 1
        m = (m ^ (m << j)) & 0xFFFFFFFF
    return x


def _attn_kernel(k_top, qit_ref, kib_ref, wt_ref, qlt_ref, cb_ref, cbt_ref, cm_ref, cmt_ref, ga_ref,
                 tbd_ref, tbp_ref, tm0_ref, bfar_ref, wuvt_ref, a_ref,
                 plane_scr, sel_scr, tie_scr, am_scr, lg_scr, oacc_scr):
    i = pl.program_id(1)
    nch = i // KEY_TILES + 1
    wide = ATTN_HEADS * QB
    prows = plane_scr.shape[0] // WORD_BITS

    def rows_of(c):
        return pl.ds(pl.multiple_of(c * CK, CK), CK)

    def tile_rows(j):
        return pl.ds(pl.multiple_of(j * QB, QB), QB)

    def group_word(ref, g):
        return ref[pl.ds(pl.multiple_of(g * SUBLANES, SUBLANES), SUBLANES), :]

    qit = qit_ref[...]
    wt = wt_ref[...]
    k_loc = lax.broadcasted_iota(jnp.int32, (CK, QB), 0)
    q_pos = i * QB + lax.broadcasted_iota(jnp.int32, (CK, QB), 1)

    def score_body(c, carry):
        kc = kib_ref[rows_of(c), :]
        s = jnp.maximum(_dot(kc, qit[0:IDX_DIM]), 0.0) * wt[0:1]
        for h in range(1, IDX_HEADS):
            s = s + jnp.maximum(_dot(kc, qit[h * IDX_DIM:(h + 1) * IDX_DIM]), 0.0) * wt[h:h + 1]
        u = _to_ukey(jnp.where(c * CK + k_loc <= q_pos, s, -jnp.inf))
        for gl in range(CK // GROUP):
            words = [u[gl * GROUP + j * SUBLANES:gl * GROUP + (j + 1) * SUBLANES] for j in range(WORD_BITS)]
            g = c * (CK // GROUP) + gl
            for p, word in enumerate(_bit_transpose(words)):
                plane_scr[pl.ds(pl.multiple_of(p * prows + g * SUBLANES, SUBLANES), SUBLANES), :] = word
        return carry

    lax.fori_loop(0, nch, score_body, 0)

    group = lax.broadcasted_iota(jnp.int32, (prows, QB), 0) // SUBLANES
    alive0 = jnp.where(group < nch * (CK // GROUP), -1, 0)

    def ones_per_query(x):
        return jnp.sum(_fold_rows(lax.population_count(x), jnp.add).astype(F32), axis=0, keepdims=True)

    def bit_body(p, carry):
        alive, chosen, want, thr = carry
        hi = alive & plane_scr[pl.ds(pl.multiple_of(p * prows, prows), prows), :]
        n_hi = ones_per_query(hi)
        up = n_hi >= want
        return (jnp.where(up, hi, alive ^ hi), chosen | jnp.where(up, 0, hi),
                jnp.where(up, want, want - n_hi),
                thr | jnp.where(up, jnp.left_shift(jnp.int32(1), WORD_BITS - 1 - p), 0))

    alive, chosen, want, thr = lax.fori_loop(
        0, WORD_BITS, bit_body,
        (alive0, jnp.zeros((prows, QB), jnp.int32), jnp.full((1, QB), float(k_top), F32),
         jnp.zeros((1, QB), jnp.int32)))

    need = jnp.where(thr == (NEG_INF_KEY ^ INT_MIN), 0.0, want)
    n_tie = ones_per_query(alive)
    sel_scr[...] = chosen | jnp.where(n_tie <= need, alive, 0)
    tie_scr[...] = alive

    def unpack(word, j):
        return jnp.left_shift(word, j) < 0

    def mask_body(c, carry):
        for gl in range(CK // GROUP):
            g = c * (CK // GROUP) + gl
            word = group_word(sel_scr, g)
            am_scr[pl.ds(pl.multiple_of(g * GROUP, GROUP), GROUP), :] = jnp.concatenate(
                [jnp.where(unpack(word, j), 0.0, -jnp.inf) for j in range(WORD_BITS)], axis=0)
        return carry

    lax.fori_loop(0, nch, mask_body, 0)

    partial_ties = jnp.max(jnp.where((need > 0.0) & (need < n_tie), 1.0, 0.0)) > 0.5

    @pl.when(partial_ties)
    def _():
        tri = jnp.where(lax.broadcasted_iota(jnp.int32, (QB, QB), 1)
                        <= lax.broadcasted_iota(jnp.int32, (QB, QB), 0), 1.0, 0.0).astype(BF16)
        vregs_per_tile = QB // SUBLANES

        def rank_body(j, run):
            g = j // (GROUP // QB)
            first = (j % (GROUP // QB)) * vregs_per_tile
            word_sel = group_word(sel_scr, g)
            word_tie = group_word(tie_scr, g)
            eq = jnp.concatenate(
                [jnp.where(jnp.left_shift(word_tie, first + v) < 0, 1.0, 0.0) for v in range(vregs_per_tile)], axis=0)
            got = jnp.concatenate(
                [jnp.where(jnp.left_shift(word_sel, first + v) < 0, 1.0, 0.0) for v in range(vregs_per_tile)], axis=0)
            pre = _dot(tri, eq.astype(BF16))
            take = (got > 0.5) | ((eq > 0.5) & (run + pre <= need))
            am_scr[tile_rows(j), :] = jnp.where(take, 0.0, -jnp.inf)
            return run + pre[QB - 1:QB, :]

        lax.fori_loop(0, nch * KEY_TILES, rank_body, jnp.zeros((1, QB), F32))

    qt = jnp.concatenate([qlt_ref[h] for h in range(ATTN_HEADS)], axis=1)
    bfar = bfar_ref[0:1, :]

    def masked(lg, rows):
        am = am_scr[rows, :]
        return lg + jnp.concatenate([am] * ATTN_HEADS, axis=1)

    def logit_body(c, m):
        lg = masked(_dot(cb_ref[rows_of(c), :], qt) + bfar, rows_of(c))
        lg_scr[rows_of(c), :] = lg
        for t in range(KEY_TILES):
            j = c * KEY_TILES + t
            mt = _fold_rows(lg[t * QB:(t + 1) * QB], jnp.maximum)
            m = jnp.maximum(m, jnp.where((j == i) | (j == i - 1), -jnp.inf, mt))
        return m

    m = lax.fori_loop(0, nch, logit_body, jnp.full((SUBLANES, wide), -jnp.inf, F32))

    jp = jnp.maximum(i - 1, 0)
    lgp = masked(_dot(cb_ref[tile_rows(jp), :], qt) + tbp_ref[...], tile_rows(jp))
    lgp = lgp + jnp.where(i >= 1, 0.0, -jnp.inf)
    lg_scr[tile_rows(jp), :] = lgp
    lgd = masked(_dot(cb_ref[tile_rows(i), :], qt) + tbd_ref[...], tile_rows(i))
    lg_scr[tile_rows(i), :] = lgd
    lgm = _dot(cm_ref[...], qt) + jnp.where(i == 0, tm0_ref[...], bfar)
    m = jnp.maximum(m, jnp.maximum(_fold_rows(lgp, jnp.maximum), _fold_rows(lgd, jnp.maximum)))
    m_row = jnp.maximum(jnp.max(m, axis=0, keepdims=True), jnp.max(lgm, axis=0, keepdims=True))

    oacc_scr[...] = jnp.zeros_like(oacc_scr)

    def pv_body(c, l):
        pr = jnp.exp(lg_scr[rows_of(c), :] - m_row)
        ct = cbt_ref[pl.ds(c * KEY_TILES, KEY_TILES)]
        ct = jnp.concatenate([ct[t] for t in range(KEY_TILES)], axis=1)
        oacc_scr[...] += _dot(ct, pr.astype(BF16))
        return l + _fold_rows(pr, jnp.add)

    l = lax.fori_loop(0, nch, pv_body, jnp.zeros((SUBLANES, wide), F32))
    pm = jnp.exp(lgm - m_row)
    l_row = jnp.sum(l, axis=0, keepdims=True) + jnp.sum(pm, axis=0, keepdims=True)
    o = ((oacc_scr[...] + _dot(cmt_ref[...], pm.astype(BF16))) / l_row).astype(BF16)
    o_all = jnp.concatenate([o[:, h * QB:(h + 1) * QB] for h in range(ATTN_HEADS)], axis=0)
    a_t = _dot(wuvt_ref[...], o_all)
    a_ref[...] = a_t.T * ga_ref[...]


def _attention(k_top, batch, seq, qit, kib, wt, qlt, cb, cbt, cm, cmt, ga, tbd, tbp, tm0, bfar, wuvt):
    nq = seq // QB
    full = lambda a: pl.BlockSpec(a.shape, lambda b, i: (0,) * a.ndim)
    qcols = lambda n: pl.BlockSpec((n, QB), lambda b, i: (0, b * nq + i))
    per_batch = lambda w: pl.BlockSpec((seq, w), lambda b, i: (b, 0))
    return pl.pallas_call(
        functools.partial(_attn_kernel, k_top),
        grid=(batch, nq),
        in_specs=[
            qcols(IDX_HEADS * IDX_DIM),
            per_batch(IDX_DIM),
            qcols(SUBLANES),
            pl.BlockSpec((ATTN_HEADS, KV_RANK, QB), lambda b, i: (0, 0, b * nq + i)),
            per_batch(KV_RANK),
            pl.BlockSpec((nq, KV_RANK, QB), lambda b, i: (b, 0, 0)),
            full(cm), full(cmt),
            pl.BlockSpec((QB, ATTN_WIDTH), lambda b, i: (b * nq + i, 0)),
            full(tbd), full(tbp), full(tm0), full(bfar), full(wuvt),
        ],
        out_specs=pl.BlockSpec((QB, ATTN_WIDTH), lambda b, i: (b * nq + i, 0)),
        out_shape=jax.ShapeDtypeStruct((batch * seq, ATTN_WIDTH), F32),
        scratch_shapes=[
            pltpu.VMEM((seq, QB), jnp.int32),
            pltpu.VMEM((seq // WORD_BITS, QB), jnp.int32),
            pltpu.VMEM((seq // WORD_BITS, QB), jnp.int32),
            pltpu.VMEM((seq, QB), F32),
            pltpu.VMEM((seq, ATTN_HEADS * QB), F32),
            pltpu.VMEM((KV_RANK, ATTN_HEADS * QB), F32),
        ],
        compiler_params=_params("arbitrary", "arbitrary"),
        name="dsa_attention",
    )(qit, kib, wt, qlt, cb, cbt, cm, cmt, ga, tbd, tbp, tm0, bfar, wuvt)


def _meta_attn_kernel(qlt_ref, cm_ref, cmt_ref, ga_ref, tbd_ref, wuvt_ref, a_ref, o_scr):
    cm = cm_ref[...]
    cmt = cmt_ref[...]
    k = lax.broadcasted_iota(jnp.int32, (N_META, N_META), 0)
    q = lax.broadcasted_iota(jnp.int32, (N_META, N_META), 1)
    for h in range(ATTN_HEADS):
        lg = _dot(cm, qlt_ref[h]) + tbd_ref[:N_META, h * QB:h * QB + N_META]
        lg = jnp.where(k <= q, lg, -jnp.inf)
        pr = jnp.exp(lg - jnp.max(lg, axis=0, keepdims=True))
        pr = pr / jnp.sum(pr, axis=0, keepdims=True)
        o_scr[h * KV_RANK:(h + 1) * KV_RANK, :] = _dot(cmt, pr.astype(BF16)).astype(BF16)
    a_ref[...] = _dot(wuvt_ref[...], o_scr[...]).T * ga_ref[...]


def _meta_attention(qlt_m, cm, cmt, ga_m, tbd, wuvt):
    return pl.pallas_call(
        _meta_attn_kernel,
        out_shape=jax.ShapeDtypeStruct((N_META, ATTN_WIDTH), F32),
        scratch_shapes=[pltpu.VMEM((ATTN_HEADS * KV_RANK, N_META), BF16)],
        name="meta_attention",
    )(qlt_m, cm, cmt, ga_m, tbd, wuvt)


def _hgrn_levels(chunk):
    return [1 << p for p in range(int(math.log2(chunk)))]


def _hgrn_masks(chunk):
    t = np.arange(chunk)[:, None]
    s = np.arange(chunk)[None, :]
    masks = [t == s]
    for m in _hgrn_levels(chunk):
        same = (t // (2 * m)) == (s // (2 * m))
        masks.append(same & ((t & m) != 0) & ((s & m) == 0))
    return np.stack(masks).astype(np.float32)


def _block_ref_rows(b, m, chunk, row):
    if 2 * m >= SUBLANES:
        pieces = []
        for blk in range(chunk // (2 * m)):
            r = blk * 2 * m + m - 1
            pieces.append(jnp.broadcast_to(b[r:r + 1, :], (2 * m, b.shape[1])))
        return pieces[0] if len(pieces) == 1 else jnp.concatenate(pieces, axis=0)
    delta = (row & (2 * m - 1)) - (m - 1)
    out = b
    for d in range(-(m - 1), m + 1):
        if d != 0:
            out = jnp.where(delta == d, pltpu.roll(b, d % chunk, 0), out)
    return out


def _split3(x):
    hi = x.astype(BF16)
    r1 = x - hi.astype(F32)
    mid = r1.astype(BF16)
    lo = (r1 - mid.astype(F32)).astype(BF16)
    return hi, mid, lo


def _hgrn_kernel(chunk, qf_ref, g_ref, kk_ref, v_ref, gh_ref, s0_ref, ng_ref, tri_ref, lm_ref,
                 r_ref, sfin_ref, st_scr):
    ci = pl.program_id(1)

    @pl.when(ci == 0)
    def _():
        st_scr[...] = s0_ref[...]

    tri = tri_ref[...]
    row = lax.broadcasted_iota(jnp.int32, (chunk, HGRN_EXPAND), 0)
    levels = _hgrn_levels(chunk)
    for hd in range(HGRN_HEADS):
        sl = slice(hd * HGRN_EXPAND, (hd + 1) * HGRN_EXPAND)
        q = qf_ref[:, sl]
        g = g_ref[:, sl]
        k = kk_ref[:, sl]
        vb = v_ref[:, sl].astype(BF16)
        g_hi, g_mid, g_lo = _split3(g)
        b = _dot(tri, g_hi) + _dot(tri, g_mid) + _dot(tri, g_lo)

        a = _dot_nt(q.astype(BF16), k.astype(BF16)) * lm_ref[0]
        for li, m in enumerate(levels):
            bref = _block_ref_rows(b, m, chunk, row)
            e = jnp.exp(jnp.where((row & m) != 0, b - bref, bref - b))
            a = a + _dot_nt((q * e).astype(BF16), (k * e).astype(BF16)) * lm_ref[li + 1]

        st = st_scr[hd]
        o = _dot_nt((q * jnp.exp(b)).astype(BF16), st.astype(BF16)) + _dot(a.astype(BF16), vb)
        b_last = b[chunk - 1:chunk, :]
        kd = (k * jnp.exp(b_last - b)).astype(BF16)
        st_scr[hd] = st * jnp.exp(b_last) + _dot_tn(vb, kd)

        rn = o * lax.rsqrt(jnp.mean(o * o, axis=-1, keepdims=True) + EPS) * ng_ref[...]
        r_ref[:, sl] = rn * gh_ref[:, sl]

    @pl.when(ci == pl.num_programs(1) - 1)
    def _():
        sfin_ref[0] = st_scr[...]


def _hgrn(batch, seq, chunk, qf, g, kk, v, gh, s0, ng):
    nc = seq // chunk
    tri = jnp.asarray(np.tril(np.ones((chunk, chunk), np.float32)), BF16)
    lm = jnp.asarray(_hgrn_masks(chunk))
    blk = pl.BlockSpec((chunk, HGRN_WIDTH), lambda b, c: (b * nc + c, 0))
    full = lambda a: pl.BlockSpec(a.shape, lambda b, c: (0,) * a.ndim)
    return pl.pallas_call(
        functools.partial(_hgrn_kernel, chunk),
        grid=(batch, nc),
        in_specs=[blk, blk, blk, blk, blk, full(s0), full(ng), full(tri), full(lm)],
        out_specs=(blk, pl.BlockSpec((1, HGRN_HEADS, HGRN_EXPAND, HGRN_EXPAND),
                                     lambda b, c: (b, 0, 0, 0))),
        out_shape=(jax.ShapeDtypeStruct((batch * seq, HGRN_WIDTH), F32),
                   jax.ShapeDtypeStruct((batch, HGRN_HEADS, HGRN_EXPAND, HGRN_EXPAND), F32)),
        scratch_shapes=[pltpu.VMEM((HGRN_HEADS, HGRN_EXPAND, HGRN_EXPAND), F32)],
        compiler_params=_params("arbitrary", "arbitrary"),
        name=f"hgrn2_c{chunk}",
    )(qf, g, kk, v, gh, s0, ng, tri, lm)


def _out_kernel(a_ref, r_ref, h_ref, wo_ref, lg_ref, lb_ref, o_ref):
    y = (_dot(a_ref[...].astype(BF16), wo_ref[:ATTN_WIDTH, :])
         + _dot(r_ref[...].astype(BF16), wo_ref[ATTN_WIDTH:, :]))
    z = DN_ALPHA * h_ref[...] + y
    mu = jnp.mean(z, axis=-1, keepdims=True)
    zc = z - mu
    var = jnp.mean(zc * zc, axis=-1, keepdims=True)
    o_ref[...] = zc * lax.rsqrt(var + EPS) * lg_ref[...] + lb_ref[...]


def _out_project(a, r, h2d, wo, lng, lnb, rows):
    t = h2d.shape[0]
    row_spec = lambda w: pl.BlockSpec((rows, w), lambda i: (i, 0))
    full = lambda x: pl.BlockSpec(x.shape, lambda i: (0,) * x.ndim)
    return pl.pallas_call(
        _out_kernel,
        grid=(t // rows,),
        in_specs=[row_spec(ATTN_WIDTH), row_spec(HGRN_WIDTH), row_spec(D_MODEL),
                  full(wo), full(lng), full(lnb)],
        out_specs=row_spec(D_MODEL),
        out_shape=jax.ShapeDtypeStruct((t, D_MODEL), F32),
        compiler_params=_params("arbitrary"),
        name=f"out_proj_r{rows}",
    )(a, r, h2d, wo, lng, lnb)


def _block_diag(blocks):
    n, r, c = blocks.shape
    eye = jnp.eye(n, dtype=blocks.dtype)
    return (eye[:, None, :, None] * blocks[:, :, None, :]).reshape(n * r, n * c)


def kernel(x, meta_tokens, rel_bias, hgrn_lb_raw, w_in, kv_norm_g, w_uk, w_uv, hgrn_norm_g, w_out, ln_g, ln_b):
    batch, seq, _ = x.shape
    assert seq % CK == 0 and seq % HGRN_CHUNK == 0 and (batch * seq) % PROJ_ROWS == 0
    k_top = min(TOPK_MAX, seq // 4)

    rel_bias = rel_bias.astype(F32)
    tbd, tbp, tm0, bfar = _bias_tables(rel_bias)
    lbraw = hgrn_lb_raw.astype(F32)
    h = x.reshape(batch * seq, D_MODEL).astype(F32)
    hm = meta_tokens.astype(F32)
    s_zero = jnp.zeros((HGRN_HEADS, HGRN_EXPAND, HGRN_EXPAND), F32)

    wp_all = jnp.concatenate(
        [w_in[:, :, :_RAW_SPLIT], jnp.zeros((DEPTH, D_MODEL, _P_GA - _RAW_SPLIT), w_in.dtype),
         w_in[:, :, _RAW_SPLIT:]], axis=2).astype(BF16)

    for l in range(DEPTH):
        wp = wp_all[l]
        wuk_bd = _block_diag(w_uk[l]).astype(BF16)
        wuvt = _block_diag(w_uv[l]).T.astype(BF16)
        wo = w_out[l].astype(BF16)
        kvg = kv_norm_g[l].reshape(1, KV_RANK).astype(F32)
        ng = hgrn_norm_g[l].reshape(1, HGRN_EXPAND).astype(F32)
        lng = ln_g[l].reshape(1, D_MODEL).astype(F32)
        lnb = ln_b[l].reshape(1, D_MODEL).astype(F32)

        hm_pad = jnp.pad(hm, ((0, QB - N_META), (0, 0)))
        (qlt_m, cb_m, cbt_m, _, _, _, ga_m, qf_m, g_m, kk_m, v_m, gh_m) = _project(
            l, hm_pad, wp, wuk_bd, kvg, lbraw, QB)
        cm = cb_m[:N_META]
        cmt = cbt_m[0][:, :N_META]
        a_m = _meta_attention(qlt_m[:, :, :N_META], cm, cmt, ga_m[:N_META], tbd, wuvt)
        r_m, s_m = _hgrn(1, N_META, N_META, qf_m[:N_META], g_m[:N_META], kk_m[:N_META],
                         v_m[:N_META], gh_m[:N_META], s_zero, ng)

        (qlt, cb, cbt, qit, kib, wt, ga, qf, g, kk, v, gh) = _project(
            l, h, wp, wuk_bd, kvg, lbraw, PROJ_ROWS)
        a = _attention(k_top, batch, seq, qit, kib, wt, qlt, cb, cbt, cm, cmt, ga,
                       tbd, tbp, tm0, bfar, wuvt)
        r, _ = _hgrn(batch, seq, HGRN_CHUNK, qf, g, kk, v, gh, s_m[0], ng)

        hm = _out_project(a_m, r_m, hm, wo, lng, lnb, N_META)
        h = _out_project(a, r, h, wo, lng, lnb, PROJ_ROWS)

    return h.reshape(batch, seq, D_MODEL).astype(x.dtype)
```

```python
import functools
import math

import numpy as np
import jax
import jax.numpy as jnp
from jax import lax
from jax.experimental import pallas as pl
from jax.experimental.pallas import tpu as pltpu

D_MODEL = 1024
DEPTH = 2
N_META = 16
ATTN_WIDTH = 512
HGRN_WIDTH = 512
ATTN_HEADS = 8
ATTN_HEAD_DIM = 64
KV_RANK = 128
IDX_HEADS = 4
IDX_DIM = 64
TOPK_MAX = 256
HGRN_EXPAND = 128
HGRN_HEADS = 4
REL_BUCKETS = 32
REL_MAX_DIST = 128
DN_ALPHA = (2 * DEPTH) ** 0.25
EPS = 1e-6

F32 = jnp.float32
BF16 = jnp.bfloat16

LANES = 128
SUBLANES = 8
QB = 128
KEY_TILES = 4
CK = KEY_TILES * QB
WORD_BITS = 32
GROUP = WORD_BITS * SUBLANES
HGRN_CHUNK = 128
PROJ_ROWS = 256
VMEM_LIMIT = 56 * 1024 * 1024

_P_QA, _P_CKV, _P_QI, _P_KW, _P_GA, _P_QH, _P_FH, _P_IH, _P_GH, _P_END = (
    0, 512, 640, 896, 1024, 1536, 2048, 2560, 3072, 3584)
_RAW_SPLIT = 964

NEG_INF_KEY = -2139095041
_WORD_INDEX_LOW = (0xFFFF0000, 0xFF00FF00, 0xF0F0F0F0, 0xCCCCCCCC, 0xAAAAAAAA)
_WORD_BIT = [int(np.uint32(1 << (31 - j)).astype(np.int32)) for j in range(32)]
INT_MIN = -2147483648


def _bucket_starts():
    max_exact = REL_BUCKETS // 2
    n = np.arange(0, 4 * REL_MAX_DIST)
    nf = np.maximum(n, 1).astype(np.float64)
    large = max_exact + (np.log(nf / max_exact) / math.log(REL_MAX_DIST / max_exact)
                         * (REL_BUCKETS - max_exact)).astype(np.int64)
    large = np.minimum(large, REL_BUCKETS - 1)
    bucket = np.where(n < max_exact, n, large)
    return [int(np.argmax(bucket >= b)) for b in range(REL_BUCKETS)]


_BUCKET_START = _bucket_starts()


def _dot(a, b):
    return jnp.dot(a, b, preferred_element_type=F32)


def _dot_nt(a, b):
    return lax.dot_general(a, b, (((1,), (1,)), ((), ())), preferred_element_type=F32)


def _dot_tn(a, b):
    return lax.dot_general(a, b, (((0,), (0,)), ((), ())), preferred_element_type=F32)


def _sigmoid(x):
    return 1.0 / (1.0 + jnp.exp(-x))


def _params(*sem):
    return pltpu.CompilerParams(dimension_semantics=sem, vmem_limit_bytes=VMEM_LIMIT)


def _fold_rows(x, op):
    parts = [x[r:r + SUBLANES] for r in range(0, x.shape[0], SUBLANES)]
    while len(parts) > 1:
        pairs = [op(parts[k], parts[k + 1]) for k in range(0, len(parts) - 1, 2)]
        parts = pairs + parts[len(parts) - len(parts) % 2:]
    return parts[0]


def _proj_kernel(layer, h_ref, wp_ref, wuk_ref, kvg_ref, lbraw_ref,
                 qlt_ref, cb_ref, cbt_ref, qit_ref, kib_ref, wt_ref, ga_ref,
                 qf_ref, g_ref, kk_ref, v_ref, gh_ref):
    rows = h_ref.shape[0]
    hb = h_ref[...].astype(BF16)

    def proj(lo, hi):
        return _dot(hb, wp_ref[:, lo:hi])

    ql = _dot(proj(_P_QA, _P_CKV).astype(BF16), wuk_ref[...]) * (ATTN_HEAD_DIM ** -0.5)
    qlt_ref[...] = ql.T.reshape(ATTN_HEADS, KV_RANK, rows).astype(BF16)

    ckv = proj(_P_CKV, _P_QI)
    c = ckv * lax.rsqrt(jnp.mean(ckv * ckv, axis=-1, keepdims=True) + EPS) * kvg_ref[...]
    cb_ref[...] = c.astype(BF16)
    ct = c.T
    for t in range(rows // QB):
        cbt_ref[t] = ct[:, t * QB:(t + 1) * QB].astype(BF16)

    qit_ref[...] = proj(_P_QI, _P_KW).T.astype(BF16)
    kw = proj(_P_KW, _P_GA)
    kib_ref[...] = kw[:, :IDX_DIM].astype(BF16)
    wt_ref[...] = kw.T[IDX_DIM:IDX_DIM + SUBLANES, :] * (IDX_HEADS ** -0.5 * IDX_DIM ** -0.5)

    ga = proj(_P_GA, _P_QH)
    ga_ref[...] = ga * _sigmoid(ga)

    qh = proj(_P_QH, _P_FH)
    qf_ref[...] = qh * _sigmoid(qh) * (HGRN_EXPAND ** -0.5)

    raw = lbraw_ref[...]
    ex = jnp.exp(raw - jnp.max(raw, axis=0, keepdims=True))
    lbp = ex / jnp.sum(ex, axis=0, keepdims=True)
    lb = lbp[0:1]
    for i in range(1, layer + 1):
        lb = lb + lbp[i:i + 1]
    lb = lb - lbp[0:1]
    f = lb + (1.0 - lb) * _sigmoid(proj(_P_FH, _P_IH))
    g_ref[...] = jnp.log(f)
    kk_ref[...] = 1.0 - f
    v_ref[...] = proj(_P_IH, _P_GH)
    gh = proj(_P_GH, _P_END)
    gh_ref[...] = gh * _sigmoid(gh)


def _project(layer, h2d, wp, wuk_bd, kvg, lbraw, rows):
    t = h2d.shape[0]
    grid = (t // rows,)
    row_spec = lambda w: pl.BlockSpec((rows, w), lambda r: (r, 0))
    col_spec = lambda n: pl.BlockSpec((n, rows), lambda r: (0, r))
    full = lambda a: pl.BlockSpec(a.shape, lambda r: (0,) * a.ndim)
    out_shape = (
        jax.ShapeDtypeStruct((ATTN_HEADS, KV_RANK, t), BF16),
        jax.ShapeDtypeStruct((t, KV_RANK), BF16),
        jax.ShapeDtypeStruct((t // QB, KV_RANK, QB), BF16),
        jax.ShapeDtypeStruct((IDX_HEADS * IDX_DIM, t), BF16),
        jax.ShapeDtypeStruct((t, IDX_DIM), BF16),
        jax.ShapeDtypeStruct((SUBLANES, t), F32),
        jax.ShapeDtypeStruct((t, ATTN_WIDTH), F32),
        jax.ShapeDtypeStruct((t, HGRN_WIDTH), F32),
        jax.ShapeDtypeStruct((t, HGRN_WIDTH), F32),
        jax.ShapeDtypeStruct((t, HGRN_WIDTH), F32),
        jax.ShapeDtypeStruct((t, HGRN_WIDTH), F32),
        jax.ShapeDtypeStruct((t, HGRN_WIDTH), F32),
    )
    out_specs = (
        pl.BlockSpec((ATTN_HEADS, KV_RANK, rows), lambda r: (0, 0, r)),
        row_spec(KV_RANK),
        pl.BlockSpec((rows // QB, KV_RANK, QB), lambda r: (r, 0, 0)),
        col_spec(IDX_HEADS * IDX_DIM), row_spec(IDX_DIM), col_spec(SUBLANES),
        row_spec(ATTN_WIDTH), row_spec(HGRN_WIDTH), row_spec(HGRN_WIDTH), row_spec(HGRN_WIDTH),
        row_spec(HGRN_WIDTH), row_spec(HGRN_WIDTH),
    )
    return pl.pallas_call(
        functools.partial(_proj_kernel, layer),
        grid=grid,
        in_specs=[row_spec(D_MODEL), full(wp), full(wuk_bd), full(kvg), full(lbraw)],
        out_specs=out_specs,
        out_shape=out_shape,
        compiler_params=_params("arbitrary"),
        name=f"proj_l{layer}_r{rows}",
    )(h2d, wp, wuk_bd, kvg, lbraw)


def _bias_tile(dist, rb_ref, h):
    out = jnp.full(dist.shape, rb_ref[0, h], F32)
    for b in range(1, REL_BUCKETS):
        out = jnp.where(dist >= _BUCKET_START[b], rb_ref[b, h], out)
    return out


def _bias_kernel(rb_ref, tbd_ref, tbp_ref, tm0_ref, bfar_ref):
    k = lax.broadcasted_iota(jnp.int32, (QB, QB), 0)
    q = lax.broadcasted_iota(jnp.int32, (QB, QB), 1)
    km = lax.broadcasted_iota(jnp.int32, (N_META, QB), 0)
    qm = lax.broadcasted_iota(jnp.int32, (N_META, QB), 1)
    for h in range(ATTN_HEADS):
        lanes = slice(h * QB, (h + 1) * QB)
        tbd_ref[:, lanes] = _bias_tile(jnp.maximum(q - k, 0), rb_ref, h)
        tbp_ref[:, lanes] = _bias_tile(q - k + QB, rb_ref, h)
        tm0_ref[:, lanes] = _bias_tile(N_META + qm - km, rb_ref, h)
        bfar_ref[:, lanes] = jnp.full((SUBLANES, QB), rb_ref[REL_BUCKETS - 1, h], F32)


def _bias_tables(rel_bias):
    wide = ATTN_HEADS * QB
    return pl.pallas_call(
        _bias_kernel,
        in_specs=[pl.BlockSpec(memory_space=pltpu.SMEM)],
        out_shape=(jax.ShapeDtypeStruct((QB, wide), F32), jax.ShapeDtypeStruct((QB, wide), F32),
                   jax.ShapeDtypeStruct((N_META, wide), F32), jax.ShapeDtypeStruct((SUBLANES, wide), F32)),
        name="bias_tables",
    )(rel_bias)


def _to_ukey(s):
    bits = pltpu.bitcast(s, jnp.int32)
    return jnp.where(bits < 0, ~bits, bits | INT_MIN)


def _bit_transpose(x):
    x = list(x)
    j, m = 16, 0x0000FFFF
    while j:
        k = 0
        while k < WORD_BITS:
            t = (x[k] ^ lax.shift_right_logical(x[k + j], jnp.int32(j))) & m
            x[k] = x[k] ^ t
            x[k + j] = x[k + j] ^ jnp.left_shift(t, j)
            k = (k + j + 1) & ~j
        j >>= 1
        m = (m ^ (m << j)) & 0xFFFFFFFF
    return x


def _attn_kernel(k_top, qit_ref, kib_ref, wt_ref, qlt_ref, cb_ref, cbt_ref, cm_ref, cmt_ref, ga_ref,
                 tbd_ref, tbp_ref, tm0_ref, bfar_ref, wuvt_ref, a_ref,
                 s_scr, plane_scr, tie_scr, sel_scr, thr_scr, ngt_scr, am_scr, lg_scr, oacc_scr):
    i = pl.program_id(1)
    nch = i // KEY_TILES + 1
    wide = ATTN_HEADS * QB
    prows = plane_scr.shape[0] // WORD_BITS

    def rows_of(c):
        return pl.ds(pl.multiple_of(c * CK, CK), CK)

    def tile_rows(j):
        return pl.ds(pl.multiple_of(j * QB, QB), QB)

    def group_word(ref, g):
        return ref[pl.ds(pl.multiple_of(g * SUBLANES, SUBLANES), SUBLANES), :]

    qit = qit_ref[...]
    wt = wt_ref[...]
    k_loc = lax.broadcasted_iota(jnp.int32, (CK, QB), 0)
    q_pos = i * QB + lax.broadcasted_iota(jnp.int32, (CK, QB), 1)

    def score_body(c, carry):
        kc = kib_ref[rows_of(c), :]
        s = jnp.maximum(_dot(kc, qit[0:IDX_DIM]), 0.0) * wt[0:1]
        for h in range(1, IDX_HEADS):
            s = s + jnp.maximum(_dot(kc, qit[h * IDX_DIM:(h + 1) * IDX_DIM]), 0.0) * wt[h:h + 1]
        s = jnp.where(c * CK + k_loc <= q_pos, s, -jnp.inf)
        s_scr[rows_of(c), :] = s
        u = _to_ukey(s)
        for gl in range(CK // GROUP):
            words = [u[gl * GROUP + j * SUBLANES:gl * GROUP + (j + 1) * SUBLANES] for j in range(WORD_BITS)]
            g = c * (CK // GROUP) + gl
            for p, word in enumerate(_bit_transpose(words)):
                plane_scr[pl.ds(pl.multiple_of(p * prows + g * SUBLANES, SUBLANES), SUBLANES), :] = word
        return carry

    lax.fori_loop(0, nch, score_body, 0)

    kf = float(k_top)
    group = lax.broadcasted_iota(jnp.int32, (prows, QB), 0) // SUBLANES
    active = jnp.where(group < nch * (CK // GROUP), -1, 0)

    def ones_per_query(x):
        return jnp.sum(_fold_rows(lax.population_count(x), jnp.add).astype(F32), axis=0, keepdims=True)

    def from_ukey(u):
        return pltpu.bitcast(jnp.where(u < 0, u ^ INT_MIN, ~u), F32)

    def bit_body(p, carry):
        alive, want, thr = carry
        hi = alive & plane_scr[pl.ds(pl.multiple_of(p * prows, prows), prows), :]
        n_hi = ones_per_query(hi)
        up = n_hi >= want
        return (jnp.where(up, hi, alive ^ hi), jnp.where(up, want, want - n_hi),
                thr | jnp.where(up, jnp.left_shift(jnp.int32(1), WORD_BITS - 1 - p), 0))

    _, _, thr_bits = lax.fori_loop(
        0, WORD_BITS, bit_body,
        (active, jnp.full((1, QB), kf, F32), jnp.zeros((1, QB), jnp.int32)))

    def census(thr_f):
        def body(c, acc):
            s = s_scr[rows_of(c), :]
            for gl in range(CK // GROUP):
                bits = [jnp.where(s[gl * GROUP + j * SUBLANES:gl * GROUP + (j + 1) * SUBLANES] == thr_f,
                                  _WORD_BIT[j], 0) for j in range(WORD_BITS)]
                tie_scr[pl.ds(pl.multiple_of((c * (CK // GROUP) + gl) * SUBLANES, SUBLANES), SUBLANES), :] = (
                    _fold_rows(jnp.concatenate(bits, axis=0), jnp.bitwise_or))
            return acc + jnp.sum(jnp.where(s > thr_f, 1.0, 0.0).reshape(KEY_TILES, QB, QB), axis=0)
        acc = lax.fori_loop(0, nch, body, jnp.zeros((QB, QB), F32))
        return jnp.sum(acc, axis=0, keepdims=True)

    def keep(thr_f, n_gt):
        thr_scr[...] = jnp.broadcast_to(thr_f, (SUBLANES, QB))
        ngt_scr[...] = jnp.broadcast_to(n_gt, (SUBLANES, QB))

    thr_fast = from_ukey(thr_bits)
    n_gt_fast = census(thr_fast)
    n_eq_fast = ones_per_query(tie_scr[...] & active)
    keep(thr_fast, n_gt_fast)
    unconfirmed = (n_gt_fast >= kf) | (n_gt_fast + n_eq_fast < kf)

    @pl.when(jnp.max(jnp.where(unconfirmed, 1.0, 0.0)) > 0.5)
    def _():
        def count_ge(cand_f):
            def body(c, acc):
                hit = jnp.where(s_scr[rows_of(c), :] >= cand_f, 1.0, 0.0)
                return acc + jnp.sum(hit.reshape(KEY_TILES, QB, QB), axis=0)
            acc = lax.fori_loop(0, nch, body, jnp.zeros((QB, QB), F32))
            return jnp.sum(acc, axis=0, keepdims=True)

        def slow_body(p, prefix):
            cand = prefix | jnp.left_shift(jnp.int32(1), WORD_BITS - 1 - p)
            floor = jnp.where(cand >= 0, jnp.maximum(cand, NEG_INF_KEY ^ INT_MIN), cand)
            return jnp.where(count_ge(from_ukey(floor)) >= kf, cand, prefix)

        thr_slow = from_ukey(lax.fori_loop(0, WORD_BITS, slow_body, jnp.zeros((1, QB), jnp.int32)))
        keep(thr_slow, census(thr_slow))

    thr_f = thr_scr[0:1, :]
    n_gt = ngt_scr[0:1, :]

    rest = jnp.where(thr_f == -jnp.inf, 0.0, kf - n_gt)
    alive = tie_scr[...] & active
    row = lax.broadcasted_iota(jnp.int32, (prows, QB), 0)
    group_bits = [1 << b for b in reversed(range((prows // SUBLANES - 1).bit_length()))]
    low_half = ([jnp.where((row & (SUBLANES * gb)) == 0, -1, 0) for gb in group_bits]
                + [int(np.uint32(m).astype(np.int32)) for m in _WORD_INDEX_LOW]
                + [jnp.where((row & rb) == 0, -1, 0) for rb in (4, 2, 1)])
    ties, taken = alive, jnp.zeros((prows, QB), jnp.int32)
    for low in low_half:
        lo = ties & low
        n_lo = ones_per_query(lo)
        inside = n_lo >= rest
        taken = taken | jnp.where(inside, 0, lo)
        ties = jnp.where(inside, lo, ties ^ lo)
        rest = jnp.where(inside, rest, rest - n_lo)
    sel_scr[...] = taken | jnp.where(rest >= 1.0, ties, 0)

    def mask_body(c, carry):
        s = s_scr[rows_of(c), :]
        for gl in range(CK // GROUP):
            g = c * (CK // GROUP) + gl
            word = group_word(sel_scr, g)
            am_scr[pl.ds(pl.multiple_of(g * GROUP, GROUP), GROUP), :] = jnp.concatenate(
                [jnp.where(s[gl * GROUP + j * SUBLANES:gl * GROUP + (j + 1) * SUBLANES] > thr_f, 0.0,
                           jnp.where(jnp.left_shift(word, j) < 0, 0.0, -jnp.inf))
                 for j in range(WORD_BITS)], axis=0)
        return carry

    lax.fori_loop(0, nch, mask_body, 0)

    qt = jnp.concatenate([qlt_ref[h] for h in range(ATTN_HEADS)], axis=1)
    bfar = bfar_ref[0:1, :]

    def masked(lg, rows):
        am = am_scr[rows, :]
        return lg + jnp.concatenate([am] * ATTN_HEADS, axis=1)

    def logit_body(c, m):
        lg = masked(_dot(cb_ref[rows_of(c), :], qt) + bfar, rows_of(c))
        lg_scr[rows_of(c), :] = lg
        for t in range(KEY_TILES):
            j = c * KEY_TILES + t
            mt = _fold_rows(lg[t * QB:(t + 1) * QB], jnp.maximum)
            m = jnp.maximum(m, jnp.where((j == i) | (j == i - 1), -jnp.inf, mt))
        return m

    m = lax.fori_loop(0, nch, logit_body, jnp.full((SUBLANES, wide), -jnp.inf, F32))

    jp = jnp.maximum(i - 1, 0)
    lgp = masked(_dot(cb_ref[tile_rows(jp), :], qt) + tbp_ref[...], tile_rows(jp))
    lgp = lgp + jnp.where(i >= 1, 0.0, -jnp.inf)
    lg_scr[tile_rows(jp), :] = lgp
    lgd = masked(_dot(cb_ref[tile_rows(i), :], qt) + tbd_ref[...], tile_rows(i))
    lg_scr[tile_rows(i), :] = lgd
    lgm = _dot(cm_ref[...], qt) + jnp.where(i == 0, tm0_ref[...], bfar)
    m = jnp.maximum(m, jnp.maximum(_fold_rows(lgp, jnp.maximum), _fold_rows(lgd, jnp.maximum)))
    m_row = jnp.maximum(jnp.max(m, axis=0, keepdims=True), jnp.max(lgm, axis=0, keepdims=True))

    oacc_scr[...] = jnp.zeros_like(oacc_scr)

    def pv_body(c, l):
        pr = jnp.exp(lg_scr[rows_of(c), :] - m_row)
        ct = cbt_ref[pl.ds(c * KEY_TILES, KEY_TILES)]
        ct = jnp.concatenate([ct[t] for t in range(KEY_TILES)], axis=1)
        oacc_scr[...] += _dot(ct, pr.astype(BF16))
        return l + _fold_rows(pr, jnp.add)

    l = lax.fori_loop(0, nch, pv_body, jnp.zeros((SUBLANES, wide), F32))
    pm = jnp.exp(lgm - m_row)
    l_row = jnp.sum(l, axis=0, keepdims=True) + jnp.sum(pm, axis=0, keepdims=True)
    o = ((oacc_scr[...] + _dot(cmt_ref[...], pm.astype(BF16))) / l_row).astype(BF16)
    o_all = jnp.concatenate([o[:, h * QB:(h + 1) * QB] for h in range(ATTN_HEADS)], axis=0)
    a_t = _dot(wuvt_ref[...], o_all)
    a_ref[...] = a_t.T * ga_ref[...]


def _attention(k_top, batch, seq, qit, kib, wt, qlt, cb, cbt, cm, cmt, ga, tbd, tbp, tm0, bfar, wuvt):
    nq = seq // QB
    full = lambda a: pl.BlockSpec(a.shape, lambda b, i: (0,) * a.ndim)
    qcols = lambda n: pl.BlockSpec((n, QB), lambda b, i: (0, b * nq + i))
    per_batch = lambda w: pl.BlockSpec((seq, w), lambda b, i: (b, 0))
    return pl.pallas_call(
        functools.partial(_attn_kernel, k_top),
        grid=(batch, nq),
        in_specs=[
            qcols(IDX_HEADS * IDX_DIM),
            per_batch(IDX_DIM),
            qcols(SUBLANES),
            pl.BlockSpec((ATTN_HEADS, KV_RANK, QB), lambda b, i: (0, 0, b * nq + i)),
            per_batch(KV_RANK),
            pl.BlockSpec((nq, KV_RANK, QB), lambda b, i: (b, 0, 0)),
            full(cm), full(cmt),
            pl.BlockSpec((QB, ATTN_WIDTH), lambda b, i: (b * nq + i, 0)),
            full(tbd), full(tbp), full(tm0), full(bfar), full(wuvt),
        ],
        out_specs=pl.BlockSpec((QB, ATTN_WIDTH), lambda b, i: (b * nq + i, 0)),
        out_shape=jax.ShapeDtypeStruct((batch * seq, ATTN_WIDTH), F32),
        scratch_shapes=[
            pltpu.VMEM((seq, QB), F32),
            pltpu.VMEM((seq, QB), jnp.int32),
            pltpu.VMEM((seq // WORD_BITS, QB), jnp.int32),
            pltpu.VMEM((seq // WORD_BITS, QB), jnp.int32),
            pltpu.VMEM((SUBLANES, QB), F32),
            pltpu.VMEM((SUBLANES, QB), F32),
            pltpu.VMEM((seq, QB), F32),
            pltpu.VMEM((seq, ATTN_HEADS * QB), F32),
            pltpu.VMEM((KV_RANK, ATTN_HEADS * QB), F32),
        ],
        compiler_params=_params("arbitrary", "arbitrary"),
        name="dsa_attention",
    )(qit, kib, wt, qlt, cb, cbt, cm, cmt, ga, tbd, tbp, tm0, bfar, wuvt)


def _meta_attn_kernel(qlt_ref, cm_ref, cmt_ref, ga_ref, tbd_ref, wuvt_ref, a_ref, o_scr):
    cm = cm_ref[...]
    cmt = cmt_ref[...]
    k = lax.broadcasted_iota(jnp.int32, (N_META, N_META), 0)
    q = lax.broadcasted_iota(jnp.int32, (N_META, N_META), 1)
    for h in range(ATTN_HEADS):
        lg = _dot(cm, qlt_ref[h]) + tbd_ref[:N_META, h * QB:h * QB + N_META]
        lg = jnp.where(k <= q, lg, -jnp.inf)
        pr = jnp.exp(lg - jnp.max(lg, axis=0, keepdims=True))
        pr = pr / jnp.sum(pr, axis=0, keepdims=True)
        o_scr[h * KV_RANK:(h + 1) * KV_RANK, :] = _dot(cmt, pr.astype(BF16)).astype(BF16)
    a_ref[...] = _dot(wuvt_ref[...], o_scr[...]).T * ga_ref[...]


def _meta_attention(qlt_m, cm, cmt, ga_m, tbd, wuvt):
    return pl.pallas_call(
        _meta_attn_kernel,
        out_shape=jax.ShapeDtypeStruct((N_META, ATTN_WIDTH), F32),
        scratch_shapes=[pltpu.VMEM((ATTN_HEADS * KV_RANK, N_META), BF16)],
        name="meta_attention",
    )(qlt_m, cm, cmt, ga_m, tbd, wuvt)


def _hgrn_levels(chunk):
    return [1 << p for p in range(int(math.log2(chunk)))]


def _hgrn_masks(chunk):
    t = np.arange(chunk)[:, None]
    s = np.arange(chunk)[None, :]
    masks = [t == s]
    for m in _hgrn_levels(chunk):
        same = (t // (2 * m)) == (s // (2 * m))
        masks.append(same & ((t & m) != 0) & ((s & m) == 0))
    return np.stack(masks).astype(np.float32)


def _block_ref_rows(b, m, chunk, row):
    if 2 * m >= SUBLANES:
        pieces = []
        for blk in range(chunk // (2 * m)):
            r = blk * 2 * m + m - 1
            pieces.append(jnp.broadcast_to(b[r:r + 1, :], (2 * m, b.shape[1])))
        return pieces[0] if len(pieces) == 1 else jnp.concatenate(pieces, axis=0)
    delta = (row & (2 * m - 1)) - (m - 1)
    out = b
    for d in range(-(m - 1), m + 1):
        if d != 0:
            out = jnp.where(delta == d, pltpu.roll(b, d % chunk, 0), out)
    return out


def _split3(x):
    hi = x.astype(BF16)
    r1 = x - hi.astype(F32)
    mid = r1.astype(BF16)
    lo = (r1 - mid.astype(F32)).astype(BF16)
    return hi, mid, lo


def _hgrn_kernel(chunk, qf_ref, g_ref, kk_ref, v_ref, gh_ref, s0_ref, ng_ref, tri_ref, lm_ref,
                 r_ref, sfin_ref, st_scr):
    ci = pl.program_id(1)

    @pl.when(ci == 0)
    def _():
        st_scr[...] = s0_ref[...]

    tri = tri_ref[...]
    row = lax.broadcasted_iota(jnp.int32, (chunk, HGRN_EXPAND), 0)
    levels = _hgrn_levels(chunk)
    for hd in range(HGRN_HEADS):
        sl = slice(hd * HGRN_EXPAND, (hd + 1) * HGRN_EXPAND)
        q = qf_ref[:, sl]
        g = g_ref[:, sl]
        k = kk_ref[:, sl]
        vb = v_ref[:, sl].astype(BF16)
        g_hi, g_mid, g_lo = _split3(g)
        b = _dot(tri, g_hi) + _dot(tri, g_mid) + _dot(tri, g_lo)

        a = _dot_nt(q.astype(BF16), k.astype(BF16)) * lm_ref[0]
        for li, m in enumerate(levels):
            bref = _block_ref_rows(b, m, chunk, row)
            e = jnp.exp(jnp.where((row & m) != 0, b - bref, bref - b))
            a = a + _dot_nt((q * e).astype(BF16), (k * e).astype(BF16)) * lm_ref[li + 1]

        st = st_scr[hd]
        o = _dot_nt((q * jnp.exp(b)).astype(BF16), st.astype(BF16)) + _dot(a.astype(BF16), vb)
        b_last = b[chunk - 1:chunk, :]
        kd = (k * jnp.exp(b_last - b)).astype(BF16)
        st_scr[hd] = st * jnp.exp(b_last) + _dot_tn(vb, kd)

        rn = o * lax.rsqrt(jnp.mean(o * o, axis=-1, keepdims=True) + EPS) * ng_ref[...]
        r_ref[:, sl] = rn * gh_ref[:, sl]

    @pl.when(ci == pl.num_programs(1) - 1)
    def _():
        sfin_ref[0] = st_scr[...]


def _hgrn(batch, seq, chunk, qf, g, kk, v, gh, s0, ng):
    nc = seq // chunk
    tri = jnp.asarray(np.tril(np.ones((chunk, chunk), np.float32)), BF16)
    lm = jnp.asarray(_hgrn_masks(chunk))
    blk = pl.BlockSpec((chunk, HGRN_WIDTH), lambda b, c: (b * nc + c, 0))
    full = lambda a: pl.BlockSpec(a.shape, lambda b, c: (0,) * a.ndim)
    return pl.pallas_call(
        functools.partial(_hgrn_kernel, chunk),
        grid=(batch, nc),
        in_specs=[blk, blk, blk, blk, blk, full(s0), full(ng), full(tri), full(lm)],
        out_specs=(blk, pl.BlockSpec((1, HGRN_HEADS, HGRN_EXPAND, HGRN_EXPAND),
                                     lambda b, c: (b, 0, 0, 0))),
        out_shape=(jax.ShapeDtypeStruct((batch * seq, HGRN_WIDTH), F32),
                   jax.ShapeDtypeStruct((batch, HGRN_HEADS, HGRN_EXPAND, HGRN_EXPAND), F32)),
        scratch_shapes=[pltpu.VMEM((HGRN_HEADS, HGRN_EXPAND, HGRN_EXPAND), F32)],
        compiler_params=_params("arbitrary", "arbitrary"),
        name=f"hgrn2_c{chunk}",
    )(qf, g, kk, v, gh, s0, ng, tri, lm)


def _out_kernel(a_ref, r_ref, h_ref, wo_ref, lg_ref, lb_ref, o_ref):
    y = (_dot(a_ref[...].astype(BF16), wo_ref[:ATTN_WIDTH, :])
         + _dot(r_ref[...].astype(BF16), wo_ref[ATTN_WIDTH:, :]))
    z = DN_ALPHA * h_ref[...] + y
    mu = jnp.mean(z, axis=-1, keepdims=True)
    zc = z - mu
    var = jnp.mean(zc * zc, axis=-1, keepdims=True)
    o_ref[...] = zc * lax.rsqrt(var + EPS) * lg_ref[...] + lb_ref[...]


def _out_project(a, r, h2d, wo, lng, lnb, rows):
    t = h2d.shape[0]
    row_spec = lambda w: pl.BlockSpec((rows, w), lambda i: (i, 0))
    full = lambda x: pl.BlockSpec(x.shape, lambda i: (0,) * x.ndim)
    return pl.pallas_call(
        _out_kernel,
        grid=(t // rows,),
        in_specs=[row_spec(ATTN_WIDTH), row_spec(HGRN_WIDTH), row_spec(D_MODEL),
                  full(wo), full(lng), full(lnb)],
        out_specs=row_spec(D_MODEL),
        out_shape=jax.ShapeDtypeStruct((t, D_MODEL), F32),
        compiler_params=_params("arbitrary"),
        name=f"out_proj_r{rows}",
    )(a, r, h2d, wo, lng, lnb)


def _block_diag(blocks):
    n, r, c = blocks.shape
    eye = jnp.eye(n, dtype=blocks.dtype)
    return (eye[:, None, :, None] * blocks[:, :, None, :]).reshape(n * r, n * c)


def kernel(x, meta_tokens, rel_bias, hgrn_lb_raw, w_in, kv_norm_g, w_uk, w_uv, hgrn_norm_g, w_out, ln_g, ln_b):
    batch, seq, _ = x.shape
    assert seq % CK == 0 and seq % HGRN_CHUNK == 0 and (batch * seq) % PROJ_ROWS == 0
    assert seq // GROUP < WORD_BITS
    k_top = min(TOPK_MAX, seq // 4)

    rel_bias = rel_bias.astype(F32)
    tbd, tbp, tm0, bfar = _bias_tables(rel_bias)
    lbraw = hgrn_lb_raw.astype(F32)
    h = x.reshape(batch * seq, D_MODEL).astype(F32)
    hm = meta_tokens.astype(F32)
    s_zero = jnp.zeros((HGRN_HEADS, HGRN_EXPAND, HGRN_EXPAND), F32)

    wp_all = jnp.concatenate(
        [w_in[:, :, :_RAW_SPLIT], jnp.zeros((DEPTH, D_MODEL, _P_GA - _RAW_SPLIT), w_in.dtype),
         w_in[:, :, _RAW_SPLIT:]], axis=2).astype(BF16)

    for l in range(DEPTH):
        wp = wp_all[l]
        wuk_bd = _block_diag(w_uk[l]).astype(BF16)
        wuvt = _block_diag(w_uv[l]).T.astype(BF16)
        wo = w_out[l].astype(BF16)
        kvg = kv_norm_g[l].reshape(1, KV_RANK).astype(F32)
        ng = hgrn_norm_g[l].reshape(1, HGRN_EXPAND).astype(F32)
        lng = ln_g[l].reshape(1, D_MODEL).astype(F32)
        lnb = ln_b[l].reshape(1, D_MODEL).astype(F32)

        hm_pad = jnp.pad(hm, ((0, QB - N_META), (0, 0)))
        (qlt_m, cb_m, cbt_m, _, _, _, ga_m, qf_m, g_m, kk_m, v_m, gh_m) = _project(
            l, hm_pad, wp, wuk_bd, kvg, lbraw, QB)
        cm = cb_m[:N_META]
        cmt = cbt_m[0][:, :N_META]
        a_m = _meta_attention(qlt_m[:, :, :N_META], cm, cmt, ga_m[:N_META], tbd, wuvt)
        r_m, s_m = _hgrn(1, N_META, N_META, qf_m[:N_META], g_m[:N_META], kk_m[:N_META],
                         v_m[:N_META], gh_m[:N_META], s_zero, ng)

        (qlt, cb, cbt, qit, kib, wt, ga, qf, g, kk, v, gh) = _project(
            l, h, wp, wuk_bd, kvg, lbraw, PROJ_ROWS)
        a = _attention(k_top, batch, seq, qit, kib, wt, qlt, cb, cbt, cm, cmt, ga,
                       tbd, tbp, tm0, bfar, wuvt)
        r, _ = _hgrn(batch, seq, HGRN_CHUNK, qf, g, kk, v, gh, s_m[0], ng)

        hm = _out_project(a_m, r_m, hm, wo, lng, lnb, N_META)
        h = _out_project(a, r, h, wo, lng, lnb, PROJ_ROWS)

    return h.reshape(batch, seq, D_MODEL).astype(x.dtype)
```

```python
import functools
import math

import numpy as np
import jax
import jax.numpy as jnp
from jax import lax
from jax.experimental import pallas as pl
from jax.experimental.pallas import tpu as pltpu

D_MODEL = 1024
DEPTH = 2
N_META = 16
ATTN_WIDTH = 512
HGRN_WIDTH = 512
ATTN_HEADS = 8
ATTN_HEAD_DIM = 64
KV_RANK = 128
IDX_HEADS = 4
IDX_DIM = 64
TOPK_MAX = 256
HGRN_EXPAND = 128
HGRN_HEADS = 4
REL_BUCKETS = 32
REL_MAX_DIST = 128
DN_ALPHA = (2 * DEPTH) ** 0.25
EPS = 1e-6
LOG2E = math.log2(math.e)

F32 = jnp.float32
BF16 = jnp.bfloat16

LANES = 128
SUBLANES = 8
QB = 128
KEY_TILES = 4
CK = KEY_TILES * QB
WORD_BITS = 32
GROUP = WORD_BITS * SUBLANES
HGRN_CHUNK = 128
PROJ_ROWS = 256
OUT_ROWS = 512
VMEM_LIMIT = 56 * 1024 * 1024

_P_QA, _P_CKV, _P_QI, _P_KW, _P_GA, _P_QH, _P_FH, _P_IH, _P_GH, _P_END = (
    0, 512, 640, 896, 1024, 1536, 2048, 2560, 3072, 3584)
_RAW_SPLIT = 964

NEG_INF_KEY = -2139095041
_WORD_INDEX_LOW = (0xFFFF0000, 0xFF00FF00, 0xF0F0F0F0, 0xCCCCCCCC, 0xAAAAAAAA)
_WORD_BIT = [int(np.uint32(1 << (31 - j)).astype(np.int32)) for j in range(32)]
INT_MIN = -2147483648


def _bucket_starts():
    max_exact = REL_BUCKETS // 2
    n = np.arange(0, 4 * REL_MAX_DIST)
    nf = np.maximum(n, 1).astype(np.float64)
    large = max_exact + (np.log(nf / max_exact) / math.log(REL_MAX_DIST / max_exact)
                         * (REL_BUCKETS - max_exact)).astype(np.int64)
    large = np.minimum(large, REL_BUCKETS - 1)
    bucket = np.where(n < max_exact, n, large)
    return [int(np.argmax(bucket >= b)) for b in range(REL_BUCKETS)]


_BUCKET_START = _bucket_starts()


def _dot(a, b):
    return jnp.dot(a, b, preferred_element_type=F32)


def _dot_nt(a, b):
    return lax.dot_general(a, b, (((1,), (1,)), ((), ())), preferred_element_type=F32)


def _dot_tn(a, b):
    return lax.dot_general(a, b, (((0,), (0,)), ((), ())), preferred_element_type=F32)


def _sigmoid(x):
    return 1.0 / (1.0 + jnp.exp(-x))


def _params(*sem):
    return pltpu.CompilerParams(dimension_semantics=sem, vmem_limit_bytes=VMEM_LIMIT)


def _fold_rows(x, op):
    parts = [x[r:r + SUBLANES] for r in range(0, x.shape[0], SUBLANES)]
    while len(parts) > 1:
        pairs = [op(parts[k], parts[k + 1]) for k in range(0, len(parts) - 1, 2)]
        parts = pairs + parts[len(parts) - len(parts) % 2:]
    return parts[0]


def _proj_kernel(layer, h_ref, wp_ref, wuk_ref, kvg_ref, lbraw_ref,
                 qlt_ref, cb_ref, cbt_ref, qit_ref, kib_ref, wt_ref, ga_ref,
                 qf_ref, g_ref, kk_ref, v_ref, gh_ref):
    rows = h_ref.shape[0]
    hb = h_ref[...].astype(BF16)

    def proj(lo, hi):
        return _dot(hb, wp_ref[:, lo:hi])

    ql = _dot(proj(_P_QA, _P_CKV).astype(BF16), wuk_ref[...]) * (ATTN_HEAD_DIM ** -0.5 * LOG2E)
    qlt_ref[...] = ql.T.reshape(ATTN_HEADS, KV_RANK, rows).astype(BF16)

    ckv = proj(_P_CKV, _P_QI)
    c = ckv * lax.rsqrt(jnp.mean(ckv * ckv, axis=-1, keepdims=True) + EPS) * kvg_ref[...]
    cb_ref[...] = c.astype(BF16)
    ct = c.T
    for t in range(rows // QB):
        cbt_ref[t] = ct[:, t * QB:(t + 1) * QB].astype(BF16)

    qit_ref[...] = proj(_P_QI, _P_KW).T.astype(BF16)
    kw = proj(_P_KW, _P_GA)
    kib_ref[...] = kw[:, :IDX_DIM].astype(BF16)
    wt_ref[...] = kw.T[IDX_DIM:IDX_DIM + SUBLANES, :] * (IDX_HEADS ** -0.5 * IDX_DIM ** -0.5)

    ga = proj(_P_GA, _P_QH)
    ga_ref[...] = ga * _sigmoid(ga)

    qh = proj(_P_QH, _P_FH)
    qf_ref[...] = qh * _sigmoid(qh) * (HGRN_EXPAND ** -0.5)

    raw = lbraw_ref[...]
    ex = jnp.exp(raw - jnp.max(raw, axis=0, keepdims=True))
    lbp = ex / jnp.sum(ex, axis=0, keepdims=True)
    lb = lbp[0:1]
    for i in range(1, layer + 1):
        lb = lb + lbp[i:i + 1]
    lb = lb - lbp[0:1]
    f = lb + (1.0 - lb) * _sigmoid(proj(_P_FH, _P_IH))
    g_ref[...] = jnp.log(f)
    kk_ref[...] = 1.0 - f
    v_ref[...] = proj(_P_IH, _P_GH)
    gh = proj(_P_GH, _P_END)
    gh_ref[...] = gh * _sigmoid(gh)


def _project(layer, h2d, wp, wuk_bd, kvg, lbraw, rows):
    t = h2d.shape[0]
    grid = (t // rows,)
    row_spec = lambda w: pl.BlockSpec((rows, w), lambda r: (r, 0))
    col_spec = lambda n: pl.BlockSpec((n, rows), lambda r: (0, r))
    full = lambda a: pl.BlockSpec(a.shape, lambda r: (0,) * a.ndim)
    out_shape = (
        jax.ShapeDtypeStruct((ATTN_HEADS, KV_RANK, t), BF16),
        jax.ShapeDtypeStruct((t, KV_RANK), BF16),
        jax.ShapeDtypeStruct((t // QB, KV_RANK, QB), BF16),
        jax.ShapeDtypeStruct((IDX_HEADS * IDX_DIM, t), BF16),
        jax.ShapeDtypeStruct((t, IDX_DIM), BF16),
        jax.ShapeDtypeStruct((SUBLANES, t), F32),
        jax.ShapeDtypeStruct((t, ATTN_WIDTH), F32),
        jax.ShapeDtypeStruct((t, HGRN_WIDTH), F32),
        jax.ShapeDtypeStruct((t, HGRN_WIDTH), F32),
        jax.ShapeDtypeStruct((t, HGRN_WIDTH), F32),
        jax.ShapeDtypeStruct((t, HGRN_WIDTH), F32),
        jax.ShapeDtypeStruct((t, HGRN_WIDTH), F32),
    )
    out_specs = (
        pl.BlockSpec((ATTN_HEADS, KV_RANK, rows), lambda r: (0, 0, r)),
        row_spec(KV_RANK),
        pl.BlockSpec((rows // QB, KV_RANK, QB), lambda r: (r, 0, 0)),
        col_spec(IDX_HEADS * IDX_DIM), row_spec(IDX_DIM), col_spec(SUBLANES),
        row_spec(ATTN_WIDTH), row_spec(HGRN_WIDTH), row_spec(HGRN_WIDTH), row_spec(HGRN_WIDTH),
        row_spec(HGRN_WIDTH), row_spec(HGRN_WIDTH),
    )
    return pl.pallas_call(
        functools.partial(_proj_kernel, layer),
        grid=grid,
        in_specs=[row_spec(D_MODEL), full(wp), full(wuk_bd), full(kvg), full(lbraw)],
        out_specs=out_specs,
        out_shape=out_shape,
        compiler_params=_params("arbitrary"),
        name=f"proj_l{layer}_r{rows}",
    )(h2d, wp, wuk_bd, kvg, lbraw)


def _bias_tile(dist, rb_ref, h):
    out = jnp.full(dist.shape, rb_ref[0, h], F32)
    for b in range(1, REL_BUCKETS):
        out = jnp.where(dist >= _BUCKET_START[b], rb_ref[b, h], out)
    return out


def _bias_kernel(rb_ref, tb_ref, tm_ref):
    k = lax.broadcasted_iota(jnp.int32, (QB, QB), 0)
    q = lax.broadcasted_iota(jnp.int32, (QB, QB), 1)
    km = lax.broadcasted_iota(jnp.int32, (N_META, QB), 0)
    qm = lax.broadcasted_iota(jnp.int32, (N_META, QB), 1)
    for h in range(ATTN_HEADS):
        lanes = slice(h * QB, (h + 1) * QB)
        far = rb_ref[REL_BUCKETS - 1, h] * LOG2E
        tb_ref[0, :, lanes] = jnp.full((QB, QB), far, F32)
        tb_ref[1, :, lanes] = _bias_tile(q - k + QB, rb_ref, h) * LOG2E
        tb_ref[2, :, lanes] = _bias_tile(jnp.maximum(q - k, 0), rb_ref, h) * LOG2E
        tm_ref[0, :, lanes] = jnp.full((N_META, QB), far, F32)
        tm_ref[1, :, lanes] = _bias_tile(N_META + qm - km, rb_ref, h) * LOG2E


def _bias_tables(rel_bias):
    wide = ATTN_HEADS * QB
    return pl.pallas_call(
        _bias_kernel,
        in_specs=[pl.BlockSpec(memory_space=pltpu.SMEM)],
        out_shape=(jax.ShapeDtypeStruct((3, QB, wide), F32), jax.ShapeDtypeStruct((2, N_META, wide), F32)),
        name="bias_tables",
    )(rel_bias)


def _to_ukey(s):
    bits = pltpu.bitcast(s, jnp.int32)
    return bits ^ ((bits >> 31) | INT_MIN)


def _bit_transpose(x):
    x = list(x)
    j, m = 16, 0x0000FFFF
    while j:
        k = 0
        while k < WORD_BITS:
            t = (x[k] ^ lax.shift_right_logical(x[k + j], jnp.int32(j))) & m
            x[k] = x[k] ^ t
            x[k + j] = x[k + j] ^ jnp.left_shift(t, j)
            k = (k + j + 1) & ~j
        j >>= 1
        m = (m ^ (m << j)) & 0xFFFFFFFF
    return x


def _attn_kernel(k_top, qit_ref, kib_ref, wt_ref, qlt_ref, cb_ref, cbt_ref, cm_ref, cmt_ref, ga_ref,
                 tb_ref, tm_ref, wuvt_ref, a_ref,
                 s_scr, plane_scr, tie_scr, sel_scr, thr_scr, ngt_scr, am_scr, lg_scr, oacc_scr):
    i = pl.program_id(1)
    nch = i // KEY_TILES + 1
    wide = ATTN_HEADS * QB
    prows = plane_scr.shape[0] // WORD_BITS

    def rows_of(c):
        return pl.ds(pl.multiple_of(c * CK, CK), CK)

    def tile_rows(j):
        return pl.ds(pl.multiple_of(j * QB, QB), QB)

    def group_word(ref, g):
        return ref[pl.ds(pl.multiple_of(g * SUBLANES, SUBLANES), SUBLANES), :]

    qit = qit_ref[...]
    wt = wt_ref[...]
    k_loc = lax.broadcasted_iota(jnp.int32, (CK, QB), 0)
    q_pos = i * QB + lax.broadcasted_iota(jnp.int32, (CK, QB), 1)

    def score_chunk(c, last):
        kc = kib_ref[rows_of(c), :]
        s = jnp.maximum(_dot(kc, qit[0:IDX_DIM]), 0.0) * wt[0:1]
        for h in range(1, IDX_HEADS):
            s = s + jnp.maximum(_dot(kc, qit[h * IDX_DIM:(h + 1) * IDX_DIM]), 0.0) * wt[h:h + 1]
        if last:
            s = jnp.where(c * CK + k_loc <= q_pos, s, -jnp.inf)
        s_scr[rows_of(c), :] = s
        u = _to_ukey(s)
        for gl in range(CK // GROUP):
            words = [u[gl * GROUP + j * SUBLANES:gl * GROUP + (j + 1) * SUBLANES] for j in range(WORD_BITS)]
            g = c * (CK // GROUP) + gl
            for p, word in enumerate(_bit_transpose(words)):
                plane_scr[pl.ds(pl.multiple_of(p * prows + g * SUBLANES, SUBLANES), SUBLANES), :] = word

    def score_body(c, carry):
        score_chunk(c, False)
        return carry

    lax.fori_loop(0, nch - 1, score_body, 0)
    score_chunk(nch - 1, True)

    kf = float(k_top)
    group = lax.broadcasted_iota(jnp.int32, (prows, QB), 0) // SUBLANES
    active = jnp.where(group < nch * (CK // GROUP), -1, 0)

    def ones_per_query(x):
        return jnp.sum(_fold_rows(lax.population_count(x), jnp.add).astype(F32), axis=0, keepdims=True)

    def from_ukey(u):
        return pltpu.bitcast(jnp.where(u < 0, u ^ INT_MIN, ~u), F32)

    def bit_body(p, carry):
        alive, want, thr = carry
        hi = alive & plane_scr[pl.ds(pl.multiple_of(p * prows, prows), prows), :]
        n_hi = ones_per_query(hi)
        up = n_hi >= want
        return (jnp.where(up, hi, alive ^ hi), jnp.where(up, want, want - n_hi),
                thr | jnp.where(up, jnp.left_shift(jnp.int32(1), WORD_BITS - 1 - p), 0))

    _, _, thr_bits = lax.fori_loop(
        0, WORD_BITS, bit_body,
        (active, jnp.full((1, QB), kf, F32), jnp.zeros((1, QB), jnp.int32)))

    def census(thr_f):
        def body(c, acc):
            s = s_scr[rows_of(c), :]
            for gl in range(CK // GROUP):
                bits = [jnp.where(s[gl * GROUP + j * SUBLANES:gl * GROUP + (j + 1) * SUBLANES] == thr_f,
                                  _WORD_BIT[j], 0) for j in range(WORD_BITS)]
                tie_scr[pl.ds(pl.multiple_of((c * (CK // GROUP) + gl) * SUBLANES, SUBLANES), SUBLANES), :] = (
                    _fold_rows(jnp.concatenate(bits, axis=0), jnp.bitwise_or))
            return acc + jnp.sum(jnp.where(s > thr_f, 1.0, 0.0).reshape(KEY_TILES, QB, QB), axis=0)
        acc = lax.fori_loop(0, nch, body, jnp.zeros((QB, QB), F32))
        return jnp.sum(acc, axis=0, keepdims=True)

    def keep(thr_f, n_gt):
        thr_scr[...] = jnp.broadcast_to(thr_f, (SUBLANES, QB))
        ngt_scr[...] = jnp.broadcast_to(n_gt, (SUBLANES, QB))

    thr_fast = from_ukey(thr_bits)
    n_gt_fast = census(thr_fast)
    n_eq_fast = ones_per_query(tie_scr[...] & active)
    keep(thr_fast, n_gt_fast)
    unconfirmed = (n_gt_fast >= kf) | (n_gt_fast + n_eq_fast < kf)

    @pl.when(jnp.max(jnp.where(unconfirmed, 1.0, 0.0)) > 0.5)
    def _():
        def count_ge(cand_f):
            def body(c, acc):
                hit = jnp.where(s_scr[rows_of(c), :] >= cand_f, 1.0, 0.0)
                return acc + jnp.sum(hit.reshape(KEY_TILES, QB, QB), axis=0)
            acc = lax.fori_loop(0, nch, body, jnp.zeros((QB, QB), F32))
            return jnp.sum(acc, axis=0, keepdims=True)

        def slow_body(p, prefix):
            cand = prefix | jnp.left_shift(jnp.int32(1), WORD_BITS - 1 - p)
            floor = jnp.where(cand >= 0, jnp.maximum(cand, NEG_INF_KEY ^ INT_MIN), cand)
            return jnp.where(count_ge(from_ukey(floor)) >= kf, cand, prefix)

        thr_slow = from_ukey(lax.fori_loop(0, WORD_BITS, slow_body, jnp.zeros((1, QB), jnp.int32)))
        keep(thr_slow, census(thr_slow))

    thr_f = thr_scr[0:1, :]
    n_gt = ngt_scr[0:1, :]

    rest = jnp.where(thr_f == -jnp.inf, 0.0, kf - n_gt)
    alive = tie_scr[...] & active
    row = lax.broadcasted_iota(jnp.int32, (prows, QB), 0)
    group_bits = [1 << b for b in reversed(range((prows // SUBLANES - 1).bit_length()))]
    low_half = ([jnp.where((row & (SUBLANES * gb)) == 0, -1, 0) for gb in group_bits]
                + [int(np.uint32(m).astype(np.int32)) for m in _WORD_INDEX_LOW]
                + [jnp.where((row & rb) == 0, -1, 0) for rb in (4, 2, 1)])
    ties, taken = alive, jnp.zeros((prows, QB), jnp.int32)
    for low in low_half:
        lo = ties & low
        n_lo = ones_per_query(lo)
        inside = n_lo >= rest
        taken = taken | jnp.where(inside, 0, lo)
        ties = jnp.where(inside, lo, ties ^ lo)
        rest = jnp.where(inside, rest, rest - n_lo)
    sel_scr[...] = taken | jnp.where(rest >= 1.0, ties, 0)

    def mask_body(c, carry):
        s = s_scr[rows_of(c), :]
        for gl in range(CK // GROUP):
            g = c * (CK // GROUP) + gl
            word = group_word(sel_scr, g)
            am_scr[pl.ds(pl.multiple_of(g * GROUP, GROUP), GROUP), :] = jnp.concatenate(
                [jnp.where(s[gl * GROUP + j * SUBLANES:gl * GROUP + (j + 1) * SUBLANES] > thr_f, 0.0,
                           jnp.where(jnp.left_shift(word, j) < 0, 0.0, -jnp.inf))
                 for j in range(WORD_BITS)], axis=0)
        return carry

    lax.fori_loop(0, nch, mask_body, 0)

    qt = jnp.concatenate([qlt_ref[h] for h in range(ATTN_HEADS)], axis=1)

    def logit_body(c, m):
        lg = _dot(cb_ref[rows_of(c), :], qt)
        for t in range(KEY_TILES):
            j = c * KEY_TILES + t
            kind = jnp.where(j == i, 2, jnp.where(j == i - 1, 1, 0))
            am = am_scr[tile_rows(j), :]
            lgt = lg[t * QB:(t + 1) * QB] + tb_ref[kind] + jnp.concatenate([am] * ATTN_HEADS, axis=1)
            lg_scr[tile_rows(j), :] = lgt
            m = jnp.maximum(m, _fold_rows(lgt, jnp.maximum))
        return m

    m = lax.fori_loop(0, nch, logit_body, jnp.full((SUBLANES, wide), -jnp.inf, F32))
    lgm = _dot(cm_ref[...], qt) + tm_ref[jnp.where(i == 0, 1, 0)]
    m_row = jnp.maximum(jnp.max(m, axis=0, keepdims=True), jnp.max(lgm, axis=0, keepdims=True))

    oacc_scr[...] = jnp.zeros_like(oacc_scr)

    def pv_body(c, l):
        pr = jnp.exp2(lg_scr[rows_of(c), :] - m_row)
        ct = cbt_ref[pl.ds(c * KEY_TILES, KEY_TILES)]
        ct = jnp.concatenate([ct[t] for t in range(KEY_TILES)], axis=1)
        oacc_scr[...] += _dot(ct, pr.astype(BF16))
        return l + _fold_rows(pr, jnp.add)

    l = lax.fori_loop(0, nch, pv_body, jnp.zeros((SUBLANES, wide), F32))
    pm = jnp.exp2(lgm - m_row)
    l_row = jnp.sum(l, axis=0, keepdims=True) + jnp.sum(pm, axis=0, keepdims=True)
    o = ((oacc_scr[...] + _dot(cmt_ref[...], pm.astype(BF16))) / l_row).astype(BF16)
    o_all = jnp.concatenate([o[:, h * QB:(h + 1) * QB] for h in range(ATTN_HEADS)], axis=0)
    a_t = _dot(wuvt_ref[...], o_all)
    a_ref[...] = a_t.T * ga_ref[...]


def _attention(k_top, batch, seq, qit, kib, wt, qlt, cb, cbt, cm, cmt, ga, tb, tm, wuvt):
    nq = seq // QB
    full = lambda a: pl.BlockSpec(a.shape, lambda b, i: (0,) * a.ndim)
    qcols = lambda n: pl.BlockSpec((n, QB), lambda b, i: (0, b * nq + i))
    per_batch = lambda w: pl.BlockSpec((seq, w), lambda b, i: (b, 0))
    return pl.pallas_call(
        functools.partial(_attn_kernel, k_top),
        grid=(batch, nq),
        in_specs=[
            qcols(IDX_HEADS * IDX_DIM),
            per_batch(IDX_DIM),
            qcols(SUBLANES),
            pl.BlockSpec((ATTN_HEADS, KV_RANK, QB), lambda b, i: (0, 0, b * nq + i)),
            per_batch(KV_RANK),
            pl.BlockSpec((nq, KV_RANK, QB), lambda b, i: (b, 0, 0)),
            full(cm), full(cmt),
            pl.BlockSpec((QB, ATTN_WIDTH), lambda b, i: (b * nq + i, 0)),
            full(tb), full(tm), full(wuvt),
        ],
        out_specs=pl.BlockSpec((QB, ATTN_WIDTH), lambda b, i: (b * nq + i, 0)),
        out_shape=jax.ShapeDtypeStruct((batch * seq, ATTN_WIDTH), F32),
        scratch_shapes=[
            pltpu.VMEM((seq, QB), F32),
            pltpu.VMEM((seq, QB), jnp.int32),
            pltpu.VMEM((seq // WORD_BITS, QB), jnp.int32),
            pltpu.VMEM((seq // WORD_BITS, QB), jnp.int32),
            pltpu.VMEM((SUBLANES, QB), F32),
            pltpu.VMEM((SUBLANES, QB), F32),
            pltpu.VMEM((seq, QB), F32),
            pltpu.VMEM((seq, ATTN_HEADS * QB), F32),
            pltpu.VMEM((KV_RANK, ATTN_HEADS * QB), F32),
        ],
        compiler_params=_params("arbitrary", "arbitrary"),
        name="dsa_attention",
    )(qit, kib, wt, qlt, cb, cbt, cm, cmt, ga, tb, tm, wuvt)


def _meta_attn_kernel(qlt_ref, cm_ref, cmt_ref, ga_ref, tb_ref, wuvt_ref, a_ref, o_scr):
    cm = cm_ref[...]
    cmt = cmt_ref[...]
    k = lax.broadcasted_iota(jnp.int32, (N_META, N_META), 0)
    q = lax.broadcasted_iota(jnp.int32, (N_META, N_META), 1)
    for h in range(ATTN_HEADS):
        lg = _dot(cm, qlt_ref[h]) + tb_ref[2, :N_META, h * QB:h * QB + N_META]
        lg = jnp.where(k <= q, lg, -jnp.inf)
        pr = jnp.exp2(lg - jnp.max(lg, axis=0, keepdims=True))
        pr = pr / jnp.sum(pr, axis=0, keepdims=True)
        o_scr[h * KV_RANK:(h + 1) * KV_RANK, :] = _dot(cmt, pr.astype(BF16)).astype(BF16)
    a_ref[...] = _dot(wuvt_ref[...], o_scr[...]).T * ga_ref[...]


def _meta_attention(qlt_m, cm, cmt, ga_m, tb, wuvt):
    return pl.pallas_call(
        _meta_attn_kernel,
        out_shape=jax.ShapeDtypeStruct((N_META, ATTN_WIDTH), F32),
        scratch_shapes=[pltpu.VMEM((ATTN_HEADS * KV_RANK, N_META), BF16)],
        name="meta_attention",
    )(qlt_m, cm, cmt, ga_m, tb, wuvt)


def _hgrn_levels(chunk):
    return [1 << p for p in range(int(math.log2(chunk)))]


def _hgrn_masks(chunk):
    t = np.arange(chunk)[:, None]
    s = np.arange(chunk)[None, :]
    masks = [t == s]
    for m in _hgrn_levels(chunk):
        same = (t // (2 * m)) == (s // (2 * m))
        masks.append(same & ((t & m) != 0) & ((s & m) == 0))
    return np.stack(masks).astype(np.float32)


def _block_ref_rows(b, m, chunk, row):
    if 2 * m >= SUBLANES:
        pieces = []
        for blk in range(chunk // (2 * m)):
            r = blk * 2 * m + m - 1
            pieces.append(jnp.broadcast_to(b[r:r + 1, :], (2 * m, b.shape[1])))
        return pieces[0] if len(pieces) == 1 else jnp.concatenate(pieces, axis=0)
    delta = (row & (2 * m - 1)) - (m - 1)
    out = b
    for d in range(-(m - 1), m + 1):
        if d != 0:
            out = jnp.where(delta == d, pltpu.roll(b, d % chunk, 0), out)
    return out


def _split3(x):
    hi = x.astype(BF16)
    r1 = x - hi.astype(F32)
    mid = r1.astype(BF16)
    lo = (r1 - mid.astype(F32)).astype(BF16)
    return hi, mid, lo


def _hgrn_kernel(chunk, qf_ref, g_ref, kk_ref, v_ref, gh_ref, s0_ref, ng_ref, tri_ref, lm_ref,
                 r_ref, sfin_ref, st_scr):
    ci = pl.program_id(1)

    @pl.when(ci == 0)
    def _():
        st_scr[...] = s0_ref[...]

    tri = tri_ref[...]
    row = lax.broadcasted_iota(jnp.int32, (chunk, HGRN_EXPAND), 0)
    levels = _hgrn_levels(chunk)
    for hd in range(HGRN_HEADS):
        sl = slice(hd * HGRN_EXPAND, (hd + 1) * HGRN_EXPAND)
        q = qf_ref[:, sl]
        g = g_ref[:, sl]
        k = kk_ref[:, sl]
        vb = v_ref[:, sl].astype(BF16)
        g_hi, g_mid, g_lo = _split3(g)
        b = _dot(tri, g_hi) + _dot(tri, g_mid) + _dot(tri, g_lo)

        a = _dot_nt(q.astype(BF16), k.astype(BF16)) * lm_ref[0]
        for li, m in enumerate(levels):
            bref = _block_ref_rows(b, m, chunk, row)
            e = jnp.exp(jnp.where((row & m) != 0, b - bref, bref - b))
            a = a + _dot_nt((q * e).astype(BF16), (k * e).astype(BF16)) * lm_ref[li + 1]

        st = st_scr[hd]
        o = _dot_nt((q * jnp.exp(b)).astype(BF16), st.astype(BF16)) + _dot(a.astype(BF16), vb)
        b_last = b[chunk - 1:chunk, :]
        kd = (k * jnp.exp(b_last - b)).astype(BF16)
        st_scr[hd] = st * jnp.exp(b_last) + _dot_tn(vb, kd)

        rn = o * lax.rsqrt(jnp.mean(o * o, axis=-1, keepdims=True) + EPS) * ng_ref[...]
        r_ref[:, sl] = rn * gh_ref[:, sl]

    @pl.when(ci == pl.num_programs(1) - 1)
    def _():
        sfin_ref[0] = st_scr[...]


def _hgrn(batch, seq, chunk, qf, g, kk, v, gh, s0, ng):
    nc = seq // chunk
    tri = jnp.asarray(np.tril(np.ones((chunk, chunk), np.float32)), BF16)
    lm = jnp.asarray(_hgrn_masks(chunk))
    blk = pl.BlockSpec((chunk, HGRN_WIDTH), lambda b, c: (b * nc + c, 0))
    full = lambda a: pl.BlockSpec(a.shape, lambda b, c: (0,) * a.ndim)
    return pl.pallas_call(
        functools.partial(_hgrn_kernel, chunk),
        grid=(batch, nc),
        in_specs=[blk, blk, blk, blk, blk, full(s0), full(ng), full(tri), full(lm)],
        out_specs=(blk, pl.BlockSpec((1, HGRN_HEADS, HGRN_EXPAND, HGRN_EXPAND),
                                     lambda b, c: (b, 0, 0, 0))),
        out_shape=(jax.ShapeDtypeStruct((batch * seq, HGRN_WIDTH), F32),
                   jax.ShapeDtypeStruct((batch, HGRN_HEADS, HGRN_EXPAND, HGRN_EXPAND), F32)),
        scratch_shapes=[pltpu.VMEM((HGRN_HEADS, HGRN_EXPAND, HGRN_EXPAND), F32)],
        compiler_params=_params("arbitrary", "arbitrary"),
        name=f"hgrn2_c{chunk}",
    )(qf, g, kk, v, gh, s0, ng, tri, lm)


def _out_kernel(a_ref, r_ref, h_ref, wo_ref, lg_ref, lb_ref, o_ref):
    y = (_dot(a_ref[...].astype(BF16), wo_ref[:ATTN_WIDTH, :])
         + _dot(r_ref[...].astype(BF16), wo_ref[ATTN_WIDTH:, :]))
    z = DN_ALPHA * h_ref[...] + y
    mu = jnp.mean(z, axis=-1, keepdims=True)
    zc = z - mu
    var = jnp.mean(zc * zc, axis=-1, keepdims=True)
    o_ref[...] = zc * lax.rsqrt(var + EPS) * lg_ref[...] + lb_ref[...]


def _out_project(a, r, h2d, wo, lng, lnb, rows):
    t = h2d.shape[0]
    row_spec = lambda w: pl.BlockSpec((rows, w), lambda i: (i, 0))
    full = lambda x: pl.BlockSpec(x.shape, lambda i: (0,) * x.ndim)
    return pl.pallas_call(
        _out_kernel,
        grid=(t // rows,),
        in_specs=[row_spec(ATTN_WIDTH), row_spec(HGRN_WIDTH), row_spec(D_MODEL),
                  full(wo), full(lng), full(lnb)],
        out_specs=row_spec(D_MODEL),
        out_shape=jax.ShapeDtypeStruct((t, D_MODEL), F32),
        compiler_params=_params("arbitrary"),
        name=f"out_proj_r{rows}",
    )(a, r, h2d, wo, lng, lnb)


def _block_diag(blocks):
    n, r, c = blocks.shape
    eye = jnp.eye(n, dtype=blocks.dtype)
    return (eye[:, None, :, None] * blocks[:, :, None, :]).reshape(n * r, n * c)


def kernel(x, meta_tokens, rel_bias, hgrn_lb_raw, w_in, kv_norm_g, w_uk, w_uv, hgrn_norm_g, w_out, ln_g, ln_b):
    batch, seq, _ = x.shape
    assert seq % CK == 0 and seq % HGRN_CHUNK == 0 and (batch * seq) % OUT_ROWS == 0
    assert seq // GROUP < WORD_BITS
    k_top = min(TOPK_MAX, seq // 4)

    rel_bias = rel_bias.astype(F32)
    tb, tm = _bias_tables(rel_bias)
    lbraw = hgrn_lb_raw.astype(F32)
    h = x.reshape(batch * seq, D_MODEL).astype(F32)
    hm = meta_tokens.astype(F32)
    s_zero = jnp.zeros((HGRN_HEADS, HGRN_EXPAND, HGRN_EXPAND), F32)

    wp_all = jnp.concatenate(
        [w_in[:, :, :_RAW_SPLIT], jnp.zeros((DEPTH, D_MODEL, _P_GA - _RAW_SPLIT), w_in.dtype),
         w_in[:, :, _RAW_SPLIT:]], axis=2).astype(BF16)

    for l in range(DEPTH):
        wp = wp_all[l]
        wuk_bd = _block_diag(w_uk[l]).astype(BF16)
        wuvt = _block_diag(w_uv[l]).T.astype(BF16)
        wo = w_out[l].astype(BF16)
        kvg = kv_norm_g[l].reshape(1, KV_RANK).astype(F32)
        ng = hgrn_norm_g[l].reshape(1, HGRN_EXPAND).astype(F32)
        lng = ln_g[l].reshape(1, D_MODEL).astype(F32)
        lnb = ln_b[l].reshape(1, D_MODEL).astype(F32)

        hm_pad = jnp.pad(hm, ((0, QB - N_META), (0, 0)))
        (qlt_m, cb_m, cbt_m, _, _, _, ga_m, qf_m, g_m, kk_m, v_m, gh_m) = _project(
            l, hm_pad, wp, wuk_bd, kvg, lbraw, QB)
        cm = cb_m[:N_META]
        cmt = cbt_m[0][:, :N_META]
        a_m = _meta_attention(qlt_m[:, :, :N_META], cm, cmt, ga_m[:N_META], tb, wuvt)
        r_m, s_m = _hgrn(1, N_META, N_META, qf_m[:N_META], g_m[:N_META], kk_m[:N_META],
                         v_m[:N_META], gh_m[:N_META], s_zero, ng)

        (qlt, cb, cbt, qit, kib, wt, ga, qf, g, kk, v, gh) = _project(
            l, h, wp, wuk_bd, kvg, lbraw, PROJ_ROWS)
        a = _attention(k_top, batch, seq, qit, kib, wt, qlt, cb, cbt, cm, cmt, ga, tb, tm, wuvt)
        r, _ = _hgrn(batch, seq, HGRN_CHUNK, qf, g, kk, v, gh, s_m[0], ng)

        hm = _out_project(a_m, r_m, hm, wo, lng, lnb, N_META)
        h = _out_project(a, r, h, wo, lng, lnb, OUT_ROWS)

    return h.reshape(batch, seq, D_MODEL).astype(x.dtype)
```

```python
import functools
import math

import numpy as np
import jax
import jax.numpy as jnp
from jax import lax
from jax.experimental import pallas as pl
from jax.experimental.pallas import tpu as pltpu

D_MODEL = 1024
DEPTH = 2
N_META = 16
ATTN_WIDTH = 512
HGRN_WIDTH = 512
ATTN_HEADS = 8
ATTN_HEAD_DIM = 64
KV_RANK = 128
IDX_HEADS = 4
IDX_DIM = 64
TOPK_MAX = 256
HGRN_EXPAND = 128
HGRN_HEADS = 4
REL_BUCKETS = 32
REL_MAX_DIST = 128
DN_ALPHA = (2 * DEPTH) ** 0.25
EPS = 1e-6
LOG2E = math.log2(math.e)

F32 = jnp.float32
BF16 = jnp.bfloat16

LANES = 128
SUBLANES = 8
QB = 128
KEY_TILES = 4
CK = KEY_TILES * QB
WORD_BITS = 32
GROUP = WORD_BITS * SUBLANES
HGRN_CHUNK = 128
HGRN_STEP_CHUNKS = 4
PROJ_ROWS = 512
OUT_ROWS = 1024
VMEM_LIMIT = 56 * 1024 * 1024

_P_QA, _P_CKV, _P_QI, _P_KW, _P_GA, _P_QH, _P_FH, _P_IH, _P_GH, _P_END = (
    0, 512, 640, 896, 1024, 1536, 2048, 2560, 3072, 3584)
_RAW_SPLIT = 964

NEG_INF_KEY = -2139095041
_WORD_INDEX_LOW = (0xFFFF0000, 0xFF00FF00, 0xF0F0F0F0, 0xCCCCCCCC, 0xAAAAAAAA)
_WORD_BIT = [int(np.uint32(1 << (31 - j)).astype(np.int32)) for j in range(32)]
INT_MIN = -2147483648


def _bucket_starts():
    max_exact = REL_BUCKETS // 2
    n = np.arange(0, 4 * REL_MAX_DIST)
    nf = np.maximum(n, 1).astype(np.float64)
    large = max_exact + (np.log(nf / max_exact) / math.log(REL_MAX_DIST / max_exact)
                         * (REL_BUCKETS - max_exact)).astype(np.int64)
    large = np.minimum(large, REL_BUCKETS - 1)
    bucket = np.where(n < max_exact, n, large)
    return [int(np.argmax(bucket >= b)) for b in range(REL_BUCKETS)]


_BUCKET_START = _bucket_starts()


def _dot(a, b):
    return jnp.dot(a, b, preferred_element_type=F32)


def _dot_nt(a, b):
    return lax.dot_general(a, b, (((1,), (1,)), ((), ())), preferred_element_type=F32)


def _dot_tn(a, b):
    return lax.dot_general(a, b, (((0,), (0,)), ((), ())), preferred_element_type=F32)


def _sigmoid(x):
    return 1.0 / (1.0 + jnp.exp(-x))


def _params(*sem):
    return pltpu.CompilerParams(dimension_semantics=sem, vmem_limit_bytes=VMEM_LIMIT)


def _fold_rows(x, op):
    parts = [x[r:r + SUBLANES] for r in range(0, x.shape[0], SUBLANES)]
    while len(parts) > 1:
        pairs = [op(parts[k], parts[k + 1]) for k in range(0, len(parts) - 1, 2)]
        parts = pairs + parts[len(parts) - len(parts) % 2:]
    return parts[0]


def _proj_kernel(layer, h_ref, wp_ref, wuk_ref, kvg_ref, lbraw_ref,
                 qlt_ref, cb_ref, cbt_ref, qit_ref, kib_ref, wt_ref, ga_ref,
                 qf_ref, g_ref, kk_ref, v_ref, gh_ref):
    rows = h_ref.shape[0]
    hb = h_ref[...].astype(BF16)

    def proj(lo, hi):
        return _dot(hb, wp_ref[:, lo:hi])

    ql = _dot(proj(_P_QA, _P_CKV).astype(BF16), wuk_ref[...]) * (ATTN_HEAD_DIM ** -0.5 * LOG2E)
    qlt_ref[...] = ql.T.reshape(ATTN_HEADS, KV_RANK, rows).astype(BF16)

    ckv = proj(_P_CKV, _P_QI)
    c = ckv * lax.rsqrt(jnp.mean(ckv * ckv, axis=-1, keepdims=True) + EPS) * kvg_ref[...]
    cb_ref[...] = c.astype(BF16)
    ct = c.T
    for t in range(rows // QB):
        cbt_ref[t] = ct[:, t * QB:(t + 1) * QB].astype(BF16)

    qit_ref[...] = proj(_P_QI, _P_KW).T.astype(BF16)
    kw = proj(_P_KW, _P_GA)
    kib_ref[...] = kw[:, :IDX_DIM].astype(BF16)
    wt_ref[...] = kw.T[IDX_DIM:IDX_DIM + SUBLANES, :] * (IDX_HEADS ** -0.5 * IDX_DIM ** -0.5)

    ga = proj(_P_GA, _P_QH)
    ga_ref[...] = ga * _sigmoid(ga)

    qh = proj(_P_QH, _P_FH)
    qf_ref[...] = qh * _sigmoid(qh) * (HGRN_EXPAND ** -0.5)

    raw = lbraw_ref[...]
    ex = jnp.exp(raw - jnp.max(raw, axis=0, keepdims=True))
    lbp = ex / jnp.sum(ex, axis=0, keepdims=True)
    lb = lbp[0:1]
    for i in range(1, layer + 1):
        lb = lb + lbp[i:i + 1]
    lb = lb - lbp[0:1]
    f = lb + (1.0 - lb) * _sigmoid(proj(_P_FH, _P_IH))
    g_ref[...] = jnp.log(f)
    kk_ref[...] = 1.0 - f
    v_ref[...] = proj(_P_IH, _P_GH)
    gh = proj(_P_GH, _P_END)
    gh_ref[...] = gh * _sigmoid(gh)


def _project(layer, h2d, wp, wuk_bd, kvg, lbraw, rows):
    t = h2d.shape[0]
    grid = (t // rows,)
    row_spec = lambda w: pl.BlockSpec((rows, w), lambda r: (r, 0))
    col_spec = lambda n: pl.BlockSpec((n, rows), lambda r: (0, r))
    full = lambda a: pl.BlockSpec(a.shape, lambda r: (0,) * a.ndim)
    out_shape = (
        jax.ShapeDtypeStruct((ATTN_HEADS, KV_RANK, t), BF16),
        jax.ShapeDtypeStruct((t, KV_RANK), BF16),
        jax.ShapeDtypeStruct((t // QB, KV_RANK, QB), BF16),
        jax.ShapeDtypeStruct((IDX_HEADS * IDX_DIM, t), BF16),
        jax.ShapeDtypeStruct((t, IDX_DIM), BF16),
        jax.ShapeDtypeStruct((SUBLANES, t), F32),
        jax.ShapeDtypeStruct((t, ATTN_WIDTH), F32),
        jax.ShapeDtypeStruct((t, HGRN_WIDTH), F32),
        jax.ShapeDtypeStruct((t, HGRN_WIDTH), F32),
        jax.ShapeDtypeStruct((t, HGRN_WIDTH), F32),
        jax.ShapeDtypeStruct((t, HGRN_WIDTH), F32),
        jax.ShapeDtypeStruct((t, HGRN_WIDTH), F32),
    )
    out_specs = (
        pl.BlockSpec((ATTN_HEADS, KV_RANK, rows), lambda r: (0, 0, r)),
        row_spec(KV_RANK),
        pl.BlockSpec((rows // QB, KV_RANK, QB), lambda r: (r, 0, 0)),
        col_spec(IDX_HEADS * IDX_DIM), row_spec(IDX_DIM), col_spec(SUBLANES),
        row_spec(ATTN_WIDTH), row_spec(HGRN_WIDTH), row_spec(HGRN_WIDTH), row_spec(HGRN_WIDTH),
        row_spec(HGRN_WIDTH), row_spec(HGRN_WIDTH),
    )
    return pl.pallas_call(
        functools.partial(_proj_kernel, layer),
        grid=grid,
        in_specs=[row_spec(D_MODEL), full(wp), full(wuk_bd), full(kvg), full(lbraw)],
        out_specs=out_specs,
        out_shape=out_shape,
        compiler_params=_params("arbitrary"),
        name=f"proj_l{layer}_r{rows}",
    )(h2d, wp, wuk_bd, kvg, lbraw)


def _bias_tile(dist, rb_ref, h):
    out = jnp.full(dist.shape, rb_ref[0, h], F32)
    for b in range(1, REL_BUCKETS):
        out = jnp.where(dist >= _BUCKET_START[b], rb_ref[b, h], out)
    return out


def _bias_kernel(rb_ref, tb_ref, tm_ref):
    k = lax.broadcasted_iota(jnp.int32, (QB, QB), 0)
    q = lax.broadcasted_iota(jnp.int32, (QB, QB), 1)
    km = lax.broadcasted_iota(jnp.int32, (N_META, QB), 0)
    qm = lax.broadcasted_iota(jnp.int32, (N_META, QB), 1)
    for h in range(ATTN_HEADS):
        lanes = slice(h * QB, (h + 1) * QB)
        far = rb_ref[REL_BUCKETS - 1, h] * LOG2E
        tb_ref[0, :, lanes] = jnp.full((QB, QB), far, F32)
        tb_ref[1, :, lanes] = _bias_tile(q - k + QB, rb_ref, h) * LOG2E
        tb_ref[2, :, lanes] = _bias_tile(jnp.maximum(q - k, 0), rb_ref, h) * LOG2E
        tm_ref[0, :, lanes] = jnp.full((N_META, QB), far, F32)
        tm_ref[1, :, lanes] = _bias_tile(N_META + qm - km, rb_ref, h) * LOG2E


def _bias_tables(rel_bias):
    wide = ATTN_HEADS * QB
    return pl.pallas_call(
        _bias_kernel,
        in_specs=[pl.BlockSpec(memory_space=pltpu.SMEM)],
        out_shape=(jax.ShapeDtypeStruct((3, QB, wide), F32), jax.ShapeDtypeStruct((2, N_META, wide), F32)),
        name="bias_tables",
    )(rel_bias)


def _to_ukey(s):
    bits = pltpu.bitcast(s, jnp.int32)
    return bits ^ ((bits >> 31) | INT_MIN)


def _bit_transpose(x):
    x = list(x)
    j, m = 16, 0x0000FFFF
    while j:
        k = 0
        while k < WORD_BITS:
            t = (x[k] ^ lax.shift_right_logical(x[k + j], jnp.int32(j))) & m
            x[k] = x[k] ^ t
            x[k + j] = x[k + j] ^ jnp.left_shift(t, j)
            k = (k + j + 1) & ~j
        j >>= 1
        m = (m ^ (m << j)) & 0xFFFFFFFF
    return x


def _attn_kernel(k_top, qit_ref, kib_ref, wt_ref, qlt_ref, cb_ref, cbt_ref, cm_ref, cmt_ref, ga_ref,
                 tb_ref, tm_ref, wuvt_ref, a_ref,
                 s_scr, plane_scr, tie_scr, sel_scr, thr_scr, ngt_scr, am_scr, lg_scr, top_scr, oacc_scr):
    i = pl.program_id(1)
    nch = i // KEY_TILES + 1
    wide = ATTN_HEADS * QB
    prows = plane_scr.shape[0] // WORD_BITS

    def rows_of(c):
        return pl.ds(pl.multiple_of(c * CK, CK), CK)

    def tile_rows(j):
        return pl.ds(pl.multiple_of(j * QB, QB), QB)

    def group_word(ref, g):
        return ref[pl.ds(pl.multiple_of(g * SUBLANES, SUBLANES), SUBLANES), :]

    qit = qit_ref[...]
    wt = wt_ref[...]
    k_loc = lax.broadcasted_iota(jnp.int32, (CK, QB), 0)
    q_pos = i * QB + lax.broadcasted_iota(jnp.int32, (CK, QB), 1)

    def score_chunk(c, last):
        kc = kib_ref[rows_of(c), :]
        s = jnp.maximum(_dot(kc, qit[0:IDX_DIM]), 0.0) * wt[0:1]
        for h in range(1, IDX_HEADS):
            s = s + jnp.maximum(_dot(kc, qit[h * IDX_DIM:(h + 1) * IDX_DIM]), 0.0) * wt[h:h + 1]
        if last:
            s = jnp.where(c * CK + k_loc <= q_pos, s, -jnp.inf)
        s_scr[rows_of(c), :] = s
        u = _to_ukey(s)
        for gl in range(CK // GROUP):
            words = [u[gl * GROUP + j * SUBLANES:gl * GROUP + (j + 1) * SUBLANES] for j in range(WORD_BITS)]
            g = c * (CK // GROUP) + gl
            for p, word in enumerate(_bit_transpose(words)):
                plane_scr[pl.ds(pl.multiple_of(p * prows + g * SUBLANES, SUBLANES), SUBLANES), :] = word

    def score_body(c, carry):
        score_chunk(c, False)
        return carry

    lax.fori_loop(0, nch - 1, score_body, 0)
    score_chunk(nch - 1, True)

    kf = float(k_top)
    group = lax.broadcasted_iota(jnp.int32, (prows, QB), 0) // SUBLANES
    active = jnp.where(group < nch * (CK // GROUP), -1, 0)

    def ones_per_query(x):
        return jnp.sum(_fold_rows(lax.population_count(x), jnp.add).astype(F32), axis=0, keepdims=True)

    def from_ukey(u):
        return pltpu.bitcast(jnp.where(u < 0, u ^ INT_MIN, ~u), F32)

    def bit_body(p, carry):
        alive, want, thr = carry
        hi = alive & plane_scr[pl.ds(pl.multiple_of(p * prows, prows), prows), :]
        n_hi = ones_per_query(hi)
        up = n_hi >= want
        return (jnp.where(up, hi, alive ^ hi), jnp.where(up, want, want - n_hi),
                thr | jnp.where(up, jnp.left_shift(jnp.int32(1), WORD_BITS - 1 - p), 0))

    _, _, thr_bits = lax.fori_loop(
        0, WORD_BITS, bit_body,
        (active, jnp.full((1, QB), kf, F32), jnp.zeros((1, QB), jnp.int32)))

    def census(thr_f):
        def body(c, acc):
            s = s_scr[rows_of(c), :]
            for gl in range(CK // GROUP):
                bits = [jnp.where(s[gl * GROUP + j * SUBLANES:gl * GROUP + (j + 1) * SUBLANES] == thr_f,
                                  _WORD_BIT[j], 0) for j in range(WORD_BITS)]
                tie_scr[pl.ds(pl.multiple_of((c * (CK // GROUP) + gl) * SUBLANES, SUBLANES), SUBLANES), :] = (
                    _fold_rows(jnp.concatenate(bits, axis=0), jnp.bitwise_or))
            return acc + jnp.sum(jnp.where(s > thr_f, 1.0, 0.0).reshape(KEY_TILES, QB, QB), axis=0)
        acc = lax.fori_loop(0, nch, body, jnp.zeros((QB, QB), F32))
        return jnp.sum(acc, axis=0, keepdims=True)

    def keep(thr_f, n_gt):
        thr_scr[...] = jnp.broadcast_to(thr_f, (SUBLANES, QB))
        ngt_scr[...] = jnp.broadcast_to(n_gt, (SUBLANES, QB))

    thr_fast = from_ukey(thr_bits)
    n_gt_fast = census(thr_fast)
    n_eq_fast = ones_per_query(tie_scr[...] & active)
    keep(thr_fast, n_gt_fast)
    unconfirmed = (n_gt_fast >= kf) | (n_gt_fast + n_eq_fast < kf)

    @pl.when(jnp.max(jnp.where(unconfirmed, 1.0, 0.0)) > 0.5)
    def _():
        def count_ge(cand_f):
            def body(c, acc):
                hit = jnp.where(s_scr[rows_of(c), :] >= cand_f, 1.0, 0.0)
                return acc + jnp.sum(hit.reshape(KEY_TILES, QB, QB), axis=0)
            acc = lax.fori_loop(0, nch, body, jnp.zeros((QB, QB), F32))
            return jnp.sum(acc, axis=0, keepdims=True)

        def slow_body(p, prefix):
            cand = prefix | jnp.left_shift(jnp.int32(1), WORD_BITS - 1 - p)
            floor = jnp.where(cand >= 0, jnp.maximum(cand, NEG_INF_KEY ^ INT_MIN), cand)
            return jnp.where(count_ge(from_ukey(floor)) >= kf, cand, prefix)

        thr_slow = from_ukey(lax.fori_loop(0, WORD_BITS, slow_body, jnp.zeros((1, QB), jnp.int32)))
        keep(thr_slow, census(thr_slow))

    thr_f = thr_scr[0:1, :]
    n_gt = ngt_scr[0:1, :]

    rest = jnp.where(thr_f == -jnp.inf, 0.0, kf - n_gt)
    alive = tie_scr[...] & active
    row = lax.broadcasted_iota(jnp.int32, (prows, QB), 0)
    group_bits = [1 << b for b in reversed(range((prows // SUBLANES - 1).bit_length()))]
    low_half = ([jnp.where((row & (SUBLANES * gb)) == 0, -1, 0) for gb in group_bits]
                + [int(np.uint32(m).astype(np.int32)) for m in _WORD_INDEX_LOW]
                + [jnp.where((row & rb) == 0, -1, 0) for rb in (4, 2, 1)])
    ties, taken = alive, jnp.zeros((prows, QB), jnp.int32)
    for low in low_half:
        lo = ties & low
        n_lo = ones_per_query(lo)
        inside = n_lo >= rest
        taken = taken | jnp.where(inside, 0, lo)
        ties = jnp.where(inside, lo, ties ^ lo)
        rest = jnp.where(inside, rest, rest - n_lo)
    sel_scr[...] = taken | jnp.where(rest >= 1.0, ties, 0)

    def mask_body(c, carry):
        s = s_scr[rows_of(c), :]
        for gl in range(CK // GROUP):
            g = c * (CK // GROUP) + gl
            word = group_word(sel_scr, g)
            am_scr[pl.ds(pl.multiple_of(g * GROUP, GROUP), GROUP), :] = jnp.concatenate(
                [jnp.where(s[gl * GROUP + j * SUBLANES:gl * GROUP + (j + 1) * SUBLANES] > thr_f, 0.0,
                           jnp.where(jnp.left_shift(word, j) < 0, 0.0, -jnp.inf))
                 for j in range(WORD_BITS)], axis=0)
        return carry

    lax.fori_loop(0, nch, mask_body, 0)

    qt = jnp.concatenate([qlt_ref[h] for h in range(ATTN_HEADS)], axis=1)
    last_chunk = cb_ref.shape[0] // CK - 1

    def make_logits(slot, c):
        rows = rows_of(jnp.minimum(c, last_chunk))
        lg = _dot(cb_ref[rows, :], qt)
        top = jnp.full((SUBLANES, wide), -jnp.inf, F32)
        for t in range(KEY_TILES):
            j = c * KEY_TILES + t
            kind = jnp.where(j == i, 2, jnp.where(j == i - 1, 1, 0))
            am = jnp.where(c < nch, am_scr[tile_rows(jnp.minimum(j, (last_chunk + 1) * KEY_TILES - 1)), :],
                           -jnp.inf)
            lgt = lg[t * QB:(t + 1) * QB] + tb_ref[kind] + jnp.concatenate([am] * ATTN_HEADS, axis=1)
            lg_scr[slot, t * QB:(t + 1) * QB, :] = lgt
            top = jnp.maximum(top, _fold_rows(lgt, jnp.maximum))
        top_scr[slot] = top

    def absorb(slot, c, m_run, l_run):
        m_new = jnp.maximum(m_run, jnp.max(top_scr[slot], axis=0, keepdims=True))
        shrink = jnp.exp2(m_run - m_new)
        pr = jnp.exp2(lg_scr[slot] - m_new)
        ct = cbt_ref[pl.ds(jnp.minimum(c, last_chunk) * KEY_TILES, KEY_TILES)]
        ct = jnp.concatenate([ct[t] for t in range(KEY_TILES)], axis=1)
        oacc_scr[...] = oacc_scr[...] * shrink + _dot(ct, pr.astype(BF16))
        return m_new, l_run * shrink + jnp.sum(_fold_rows(pr, jnp.add), axis=0, keepdims=True)

    lgm = _dot(cm_ref[...], qt) + tm_ref[jnp.where(i == 0, 1, 0)]
    m_run = jnp.max(lgm, axis=0, keepdims=True)
    pm = jnp.exp2(lgm - m_run)
    l_run = jnp.sum(pm, axis=0, keepdims=True)
    oacc_scr[...] = _dot(cmt_ref[...], pm.astype(BF16))

    make_logits(0, 0)

    def pair_body(p, carry):
        make_logits(1, 2 * p + 1)
        carry = absorb(0, 2 * p, *carry)
        make_logits(0, 2 * p + 2)
        return absorb(1, 2 * p + 1, *carry)

    m_run, l_run = lax.fori_loop(0, (nch + 1) // 2, pair_body, (m_run, l_run))
    o = (oacc_scr[...] / l_run).astype(BF16)
    o_all = jnp.concatenate([o[:, h * QB:(h + 1) * QB] for h in range(ATTN_HEADS)], axis=0)
    a_t = _dot(wuvt_ref[...], o_all)
    a_ref[...] = a_t.T * ga_ref[...]


def _attention(k_top, batch, seq, qit, kib, wt, qlt, cb, cbt, cm, cmt, ga, tb, tm, wuvt):
    nq = seq // QB
    full = lambda a: pl.BlockSpec(a.shape, lambda b, i: (0,) * a.ndim)
    qcols = lambda n: pl.BlockSpec((n, QB), lambda b, i: (0, b * nq + i))
    per_batch = lambda w: pl.BlockSpec((seq, w), lambda b, i: (b, 0))
    return pl.pallas_call(
        functools.partial(_attn_kernel, k_top),
        grid=(batch, nq),
        in_specs=[
            qcols(IDX_HEADS * IDX_DIM),
            per_batch(IDX_DIM),
            qcols(SUBLANES),
            pl.BlockSpec((ATTN_HEADS, KV_RANK, QB), lambda b, i: (0, 0, b * nq + i)),
            per_batch(KV_RANK),
            pl.BlockSpec((nq, KV_RANK, QB), lambda b, i: (b, 0, 0)),
            full(cm), full(cmt),
            pl.BlockSpec((QB, ATTN_WIDTH), lambda b, i: (b * nq + i, 0)),
            full(tb), full(tm), full(wuvt),
        ],
        out_specs=pl.BlockSpec((QB, ATTN_WIDTH), lambda b, i: (b * nq + i, 0)),
        out_shape=jax.ShapeDtypeStruct((batch * seq, ATTN_WIDTH), F32),
        scratch_shapes=[
            pltpu.VMEM((seq, QB), F32),
            pltpu.VMEM((seq, QB), jnp.int32),
            pltpu.VMEM((seq // WORD_BITS, QB), jnp.int32),
            pltpu.VMEM((seq // WORD_BITS, QB), jnp.int32),
            pltpu.VMEM((SUBLANES, QB), F32),
            pltpu.VMEM((SUBLANES, QB), F32),
            pltpu.VMEM((seq, QB), F32),
            pltpu.VMEM((2, CK, ATTN_HEADS * QB), F32),
            pltpu.VMEM((2, SUBLANES, ATTN_HEADS * QB), F32),
            pltpu.VMEM((KV_RANK, ATTN_HEADS * QB), F32),
        ],
        compiler_params=_params("arbitrary", "arbitrary"),
        name="dsa_attention",
    )(qit, kib, wt, qlt, cb, cbt, cm, cmt, ga, tb, tm, wuvt)


def _meta_attn_kernel(qlt_ref, cm_ref, cmt_ref, ga_ref, tb_ref, wuvt_ref, a_ref, o_scr):
    cm = cm_ref[...]
    cmt = cmt_ref[...]
    k = lax.broadcasted_iota(jnp.int32, (N_META, N_META), 0)
    q = lax.broadcasted_iota(jnp.int32, (N_META, N_META), 1)
    for h in range(ATTN_HEADS):
        lg = _dot(cm, qlt_ref[h]) + tb_ref[2, :N_META, h * QB:h * QB + N_META]
        lg = jnp.where(k <= q, lg, -jnp.inf)
        pr = jnp.exp2(lg - jnp.max(lg, axis=0, keepdims=True))
        pr = pr / jnp.sum(pr, axis=0, keepdims=True)
        o_scr[h * KV_RANK:(h + 1) * KV_RANK, :] = _dot(cmt, pr.astype(BF16)).astype(BF16)
    a_ref[...] = _dot(wuvt_ref[...], o_scr[...]).T * ga_ref[...]


def _meta_attention(qlt_m, cm, cmt, ga_m, tb, wuvt):
    return pl.pallas_call(
        _meta_attn_kernel,
        out_shape=jax.ShapeDtypeStruct((N_META, ATTN_WIDTH), F32),
        scratch_shapes=[pltpu.VMEM((ATTN_HEADS * KV_RANK, N_META), BF16)],
        name="meta_attention",
    )(qlt_m, cm, cmt, ga_m, tb, wuvt)


def _hgrn_levels(chunk):
    return [1 << p for p in range(int(math.log2(chunk)))]


def _hgrn_masks(chunk):
    t = np.arange(chunk)[:, None]
    s = np.arange(chunk)[None, :]
    masks = [t == s]
    for m in _hgrn_levels(chunk):
        same = (t // (2 * m)) == (s // (2 * m))
        masks.append(same & ((t & m) != 0) & ((s & m) == 0))
    return np.stack(masks).astype(np.float32)


def _block_ref_rows(b, m, chunk, row):
    if 2 * m >= SUBLANES:
        pieces = []
        for blk in range(chunk // (2 * m)):
            r = blk * 2 * m + m - 1
            pieces.append(jnp.broadcast_to(b[r:r + 1, :], (2 * m, b.shape[1])))
        return pieces[0] if len(pieces) == 1 else jnp.concatenate(pieces, axis=0)
    delta = (row & (2 * m - 1)) - (m - 1)
    out = b
    for d in range(-(m - 1), m + 1):
        if d != 0:
            out = jnp.where(delta == d, pltpu.roll(b, d % chunk, 0), out)
    return out


def _split3(x):
    hi = x.astype(BF16)
    r1 = x - hi.astype(F32)
    mid = r1.astype(BF16)
    lo = (r1 - mid.astype(F32)).astype(BF16)
    return hi, mid, lo


def _hgrn_kernel(chunk, nsub, qf_ref, g_ref, kk_ref, v_ref, gh_ref, s0_ref, ng_ref, tri_ref, lm_ref,
                 r_ref, sfin_ref, st_scr):
    ci = pl.program_id(1)

    @pl.when(ci == 0)
    def _():
        st_scr[...] = s0_ref[...]

    tri = tri_ref[...]
    row = lax.broadcasted_iota(jnp.int32, (chunk, HGRN_EXPAND), 0)
    levels = _hgrn_levels(chunk)
    for hd, sub in [(hd, sub) for hd in range(HGRN_HEADS) for sub in range(nsub)]:
        sl = (slice(sub * chunk, (sub + 1) * chunk), slice(hd * HGRN_EXPAND, (hd + 1) * HGRN_EXPAND))
        q = qf_ref[sl]
        g = g_ref[sl]
        k = kk_ref[sl]
        vb = v_ref[sl].astype(BF16)
        g_hi, g_mid, g_lo = _split3(g)
        b = _dot(tri, g_hi) + _dot(tri, g_mid) + _dot(tri, g_lo)

        a = _dot_nt(q.astype(BF16), k.astype(BF16)) * lm_ref[0]
        for li, m in enumerate(levels):
            bref = _block_ref_rows(b, m, chunk, row)
            e = jnp.exp(jnp.where((row & m) != 0, b - bref, bref - b))
            a = a + _dot_nt((q * e).astype(BF16), (k * e).astype(BF16)) * lm_ref[li + 1]

        st = st_scr[hd]
        o = _dot_nt((q * jnp.exp(b)).astype(BF16), st.astype(BF16)) + _dot(a.astype(BF16), vb)
        b_last = b[chunk - 1:chunk, :]
        kd = (k * jnp.exp(b_last - b)).astype(BF16)
        st_scr[hd] = st * jnp.exp(b_last) + _dot_tn(vb, kd)

        rn = o * lax.rsqrt(jnp.mean(o * o, axis=-1, keepdims=True) + EPS) * ng_ref[...]
        r_ref[sl] = rn * gh_ref[sl]

    @pl.when(ci == pl.num_programs(1) - 1)
    def _():
        sfin_ref[0] = st_scr[...]


def _hgrn(batch, seq, chunk, nsub, qf, g, kk, v, gh, s0, ng):
    nc = seq // (chunk * nsub)
    tri = jnp.asarray(np.tril(np.ones((chunk, chunk), np.float32)), BF16)
    lm = jnp.asarray(_hgrn_masks(chunk))
    blk = pl.BlockSpec((chunk * nsub, HGRN_WIDTH), lambda b, c: (b * nc + c, 0))
    full = lambda a: pl.BlockSpec(a.shape, lambda b, c: (0,) * a.ndim)
    return pl.pallas_call(
        functools.partial(_hgrn_kernel, chunk, nsub),
        grid=(batch, nc),
        in_specs=[blk, blk, blk, blk, blk, full(s0), full(ng), full(tri), full(lm)],
        out_specs=(blk, pl.BlockSpec((1, HGRN_HEADS, HGRN_EXPAND, HGRN_EXPAND),
                                     lambda b, c: (b, 0, 0, 0))),
        out_shape=(jax.ShapeDtypeStruct((batch * seq, HGRN_WIDTH), F32),
                   jax.ShapeDtypeStruct((batch, HGRN_HEADS, HGRN_EXPAND, HGRN_EXPAND), F32)),
        scratch_shapes=[pltpu.VMEM((HGRN_HEADS, HGRN_EXPAND, HGRN_EXPAND), F32)],
        compiler_params=_params("arbitrary", "arbitrary"),
        name=f"hgrn2_c{chunk}x{nsub}",
    )(qf, g, kk, v, gh, s0, ng, tri, lm)


def _out_kernel(a_ref, r_ref, h_ref, wo_ref, lg_ref, lb_ref, o_ref):
    y = (_dot(a_ref[...].astype(BF16), wo_ref[:ATTN_WIDTH, :])
         + _dot(r_ref[...].astype(BF16), wo_ref[ATTN_WIDTH:, :]))
    z = DN_ALPHA * h_ref[...] + y
    mu = jnp.mean(z, axis=-1, keepdims=True)
    zc = z - mu
    var = jnp.mean(zc * zc, axis=-1, keepdims=True)
    o_ref[...] = zc * lax.rsqrt(var + EPS) * lg_ref[...] + lb_ref[...]


def _out_project(a, r, h2d, wo, lng, lnb, rows):
    t = h2d.shape[0]
    row_spec = lambda w: pl.BlockSpec((rows, w), lambda i: (i, 0))
    full = lambda x: pl.BlockSpec(x.shape, lambda i: (0,) * x.ndim)
    return pl.pallas_call(
        _out_kernel,
        grid=(t // rows,),
        in_specs=[row_spec(ATTN_WIDTH), row_spec(HGRN_WIDTH), row_spec(D_MODEL),
                  full(wo), full(lng), full(lnb)],
        out_specs=row_spec(D_MODEL),
        out_shape=jax.ShapeDtypeStruct((t, D_MODEL), F32),
        compiler_params=_params("arbitrary"),
        name=f"out_proj_r{rows}",
    )(a, r, h2d, wo, lng, lnb)


def _block_diag(blocks):
    n, r, c = blocks.shape
    eye = jnp.eye(n, dtype=blocks.dtype)
    return (eye[:, None, :, None] * blocks[:, :, None, :]).reshape(n * r, n * c)


def kernel(x, meta_tokens, rel_bias, hgrn_lb_raw, w_in, kv_norm_g, w_uk, w_uv, hgrn_norm_g, w_out, ln_g, ln_b):
    batch, seq, _ = x.shape
    assert seq % CK == 0 and seq % (HGRN_CHUNK * HGRN_STEP_CHUNKS) == 0 and (batch * seq) % OUT_ROWS == 0
    assert seq // GROUP < WORD_BITS
    k_top = min(TOPK_MAX, seq // 4)

    rel_bias = rel_bias.astype(F32)
    tb, tm = _bias_tables(rel_bias)
    lbraw = hgrn_lb_raw.astype(F32)
    h = x.reshape(batch * seq, D_MODEL).astype(F32)
    hm = meta_tokens.astype(F32)
    s_zero = jnp.zeros((HGRN_HEADS, HGRN_EXPAND, HGRN_EXPAND), F32)

    wp_all = jnp.concatenate(
        [w_in[:, :, :_RAW_SPLIT], jnp.zeros((DEPTH, D_MODEL, _P_GA - _RAW_SPLIT), w_in.dtype),
         w_in[:, :, _RAW_SPLIT:]], axis=2).astype(BF16)

    for l in range(DEPTH):
        wp = wp_all[l]
        wuk_bd = _block_diag(w_uk[l]).astype(BF16)
        wuvt = _block_diag(w_uv[l]).T.astype(BF16)
        wo = w_out[l].astype(BF16)
        kvg = kv_norm_g[l].reshape(1, KV_RANK).astype(F32)
        ng = hgrn_norm_g[l].reshape(1, HGRN_EXPAND).astype(F32)
        lng = ln_g[l].reshape(1, D_MODEL).astype(F32)
        lnb = ln_b[l].reshape(1, D_MODEL).astype(F32)

        hm_pad = jnp.pad(hm, ((0, QB - N_META), (0, 0)))
        (qlt_m, cb_m, cbt_m, _, _, _, ga_m, qf_m, g_m, kk_m, v_m, gh_m) = _project(
            l, hm_pad, wp, wuk_bd, kvg, lbraw, QB)
        cm = cb_m[:N_META]
        cmt = cbt_m[0][:, :N_META]
        a_m = _meta_attention(qlt_m[:, :, :N_META], cm, cmt, ga_m[:N_META], tb, wuvt)
        r_m, s_m = _hgrn(1, N_META, N_META, 1, qf_m[:N_META], g_m[:N_META], kk_m[:N_META],
                         v_m[:N_META], gh_m[:N_META], s_zero, ng)

        (qlt, cb, cbt, qit, kib, wt, ga, qf, g, kk, v, gh) = _project(
            l, h, wp, wuk_bd, kvg, lbraw, PROJ_ROWS)
        a = _attention(k_top, batch, seq, qit, kib, wt, qlt, cb, cbt, cm, cmt, ga, tb, tm, wuvt)
        r, _ = _hgrn(batch, seq, HGRN_CHUNK, HGRN_STEP_CHUNKS, qf, g, kk, v, gh, s_m[0], ng)

        hm = _out_project(a_m, r_m, hm, wo, lng, lnb, N_META)
        h = _out_project(a, r, h, wo, lng, lnb, OUT_ROWS)

    return h.reshape(batch, seq, D_MODEL).astype(x.dtype)
```

```python
import functools
import math

import numpy as np
import jax
import jax.numpy as jnp
from jax import lax
from jax.experimental import pallas as pl
from jax.experimental.pallas import tpu as pltpu

D_MODEL = 1024
DEPTH = 2
N_META = 16
ATTN_WIDTH = 512
HGRN_WIDTH = 512
ATTN_HEADS = 8
ATTN_HEAD_DIM = 64
KV_RANK = 128
IDX_HEADS = 4
IDX_DIM = 64
TOPK_MAX = 256
HGRN_EXPAND = 128
HGRN_HEADS = 4
REL_BUCKETS = 32
REL_MAX_DIST = 128
DN_ALPHA = (2 * DEPTH) ** 0.25
EPS = 1e-6
LOG2E = math.log2(math.e)

F32 = jnp.float32
BF16 = jnp.bfloat16

LANES = 128
SUBLANES = 8
QB = 128
KEY_TILES = 4
CK = KEY_TILES * QB
WORD_BITS = 32
GROUP = WORD_BITS * SUBLANES
PV_ROWS = KV_RANK + 16
HGRN_CHUNK = 128
HGRN_STEP_CHUNKS = 4
PROJ_ROWS = 512
OUT_ROWS = 1024
VMEM_LIMIT = 56 * 1024 * 1024

_P_QA, _P_CKV, _P_QI, _P_KW, _P_GA, _P_QH, _P_FH, _P_IH, _P_GH, _P_END = (
    0, 512, 640, 896, 1024, 1536, 2048, 2560, 3072, 3584)
_RAW_SPLIT = 964

NEG_INF_KEY = -2139095041
_WORD_INDEX_LOW = (0xFFFF0000, 0xFF00FF00, 0xF0F0F0F0, 0xCCCCCCCC, 0xAAAAAAAA)
_WORD_BIT = [int(np.uint32(1 << (31 - j)).astype(np.int32)) for j in range(32)]
INT_MIN = -2147483648


def _bucket_starts():
    max_exact = REL_BUCKETS // 2
    n = np.arange(0, 4 * REL_MAX_DIST)
    nf = np.maximum(n, 1).astype(np.float64)
    large = max_exact + (np.log(nf / max_exact) / math.log(REL_MAX_DIST / max_exact)
                         * (REL_BUCKETS - max_exact)).astype(np.int64)
    large = np.minimum(large, REL_BUCKETS - 1)
    bucket = np.where(n < max_exact, n, large)
    return [int(np.argmax(bucket >= b)) for b in range(REL_BUCKETS)]


_BUCKET_START = _bucket_starts()


def _dot(a, b):
    return jnp.dot(a, b, preferred_element_type=F32)


def _dot_nt(a, b):
    return lax.dot_general(a, b, (((1,), (1,)), ((), ())), preferred_element_type=F32)


def _dot_tn(a, b):
    return lax.dot_general(a, b, (((0,), (0,)), ((), ())), preferred_element_type=F32)


def _sigmoid(x):
    return 1.0 / (1.0 + jnp.exp(-x))


def _params(*sem):
    return pltpu.CompilerParams(dimension_semantics=sem, vmem_limit_bytes=VMEM_LIMIT)


def _fold_rows(x, op):
    parts = [x[r:r + SUBLANES] for r in range(0, x.shape[0], SUBLANES)]
    while len(parts) > 1:
        pairs = [op(parts[k], parts[k + 1]) for k in range(0, len(parts) - 1, 2)]
        parts = pairs + parts[len(parts) - len(parts) % 2:]
    return parts[0]


def _proj_kernel(layer, h_ref, wp_ref, wuk_ref, kvg_ref, lbraw_ref,
                 qlt_ref, cb_ref, cbt_ref, qit_ref, kib_ref, wt_ref, ga_ref,
                 qf_ref, g_ref, kk_ref, v_ref, gh_ref):
    rows = h_ref.shape[0]
    hb = h_ref[...].astype(BF16)

    def proj(lo, hi):
        return _dot(hb, wp_ref[:, lo:hi])

    ql = _dot(proj(_P_QA, _P_CKV).astype(BF16), wuk_ref[...]) * (ATTN_HEAD_DIM ** -0.5 * LOG2E)
    qlt_ref[...] = ql.T.reshape(ATTN_HEADS, KV_RANK, rows).astype(BF16)

    ckv = proj(_P_CKV, _P_QI)
    c = ckv * lax.rsqrt(jnp.mean(ckv * ckv, axis=-1, keepdims=True) + EPS) * kvg_ref[...]
    cb_ref[...] = c.astype(BF16)
    ct = c.T
    ones_row = jnp.where(lax.broadcasted_iota(jnp.int32, (PV_ROWS - KV_RANK, QB), 0) == 0, 1.0, 0.0)
    for t in range(rows // QB):
        cbt_ref[t] = jnp.concatenate([ct[:, t * QB:(t + 1) * QB], ones_row], axis=0).astype(BF16)

    qit_ref[...] = proj(_P_QI, _P_KW).T.astype(BF16)
    kw = proj(_P_KW, _P_GA)
    kib_ref[...] = kw[:, :IDX_DIM].astype(BF16)
    wt_ref[...] = kw.T[IDX_DIM:IDX_DIM + SUBLANES, :] * (IDX_HEADS ** -0.5 * IDX_DIM ** -0.5)

    ga = proj(_P_GA, _P_QH)
    ga_ref[...] = ga * _sigmoid(ga)

    qh = proj(_P_QH, _P_FH)
    qf_ref[...] = qh * _sigmoid(qh) * (HGRN_EXPAND ** -0.5)

    raw = lbraw_ref[...]
    ex = jnp.exp(raw - jnp.max(raw, axis=0, keepdims=True))
    lbp = ex / jnp.sum(ex, axis=0, keepdims=True)
    lb = lbp[0:1]
    for i in range(1, layer + 1):
        lb = lb + lbp[i:i + 1]
    lb = lb - lbp[0:1]
    f = lb + (1.0 - lb) * _sigmoid(proj(_P_FH, _P_IH))
    g_ref[...] = jnp.log(f)
    kk_ref[...] = 1.0 - f
    v_ref[...] = proj(_P_IH, _P_GH)
    gh = proj(_P_GH, _P_END)
    gh_ref[...] = gh * _sigmoid(gh)


def _project(layer, h2d, wp, wuk_bd, kvg, lbraw, rows):
    t = h2d.shape[0]
    grid = (t // rows,)
    row_spec = lambda w: pl.BlockSpec((rows, w), lambda r: (r, 0))
    col_spec = lambda n: pl.BlockSpec((n, rows), lambda r: (0, r))
    full = lambda a: pl.BlockSpec(a.shape, lambda r: (0,) * a.ndim)
    out_shape = (
        jax.ShapeDtypeStruct((ATTN_HEADS, KV_RANK, t), BF16),
        jax.ShapeDtypeStruct((t, KV_RANK), BF16),
        jax.ShapeDtypeStruct((t // QB, PV_ROWS, QB), BF16),
        jax.ShapeDtypeStruct((IDX_HEADS * IDX_DIM, t), BF16),
        jax.ShapeDtypeStruct((t, IDX_DIM), BF16),
        jax.ShapeDtypeStruct((SUBLANES, t), F32),
        jax.ShapeDtypeStruct((t, ATTN_WIDTH), F32),
        jax.ShapeDtypeStruct((t, HGRN_WIDTH), F32),
        jax.ShapeDtypeStruct((t, HGRN_WIDTH), F32),
        jax.ShapeDtypeStruct((t, HGRN_WIDTH), F32),
        jax.ShapeDtypeStruct((t, HGRN_WIDTH), F32),
        jax.ShapeDtypeStruct((t, HGRN_WIDTH), F32),
    )
    out_specs = (
        pl.BlockSpec((ATTN_HEADS, KV_RANK, rows), lambda r: (0, 0, r)),
        row_spec(KV_RANK),
        pl.BlockSpec((rows // QB, PV_ROWS, QB), lambda r: (r, 0, 0)),
        col_spec(IDX_HEADS * IDX_DIM), row_spec(IDX_DIM), col_spec(SUBLANES),
        row_spec(ATTN_WIDTH), row_spec(HGRN_WIDTH), row_spec(HGRN_WIDTH), row_spec(HGRN_WIDTH),
        row_spec(HGRN_WIDTH), row_spec(HGRN_WIDTH),
    )
    return pl.pallas_call(
        functools.partial(_proj_kernel, layer),
        grid=grid,
        in_specs=[row_spec(D_MODEL), full(wp), full(wuk_bd), full(kvg), full(lbraw)],
        out_specs=out_specs,
        out_shape=out_shape,
        compiler_params=_params("arbitrary"),
        name=f"proj_l{layer}_r{rows}",
    )(h2d, wp, wuk_bd, kvg, lbraw)


def _bias_tile(dist, rb_ref, h):
    out = jnp.full(dist.shape, rb_ref[0, h], F32)
    for b in range(1, REL_BUCKETS):
        out = jnp.where(dist >= _BUCKET_START[b], rb_ref[b, h], out)
    return out


def _bias_kernel(rb_ref, tb_ref, tm_ref):
    k = lax.broadcasted_iota(jnp.int32, (QB, QB), 0)
    q = lax.broadcasted_iota(jnp.int32, (QB, QB), 1)
    km = lax.broadcasted_iota(jnp.int32, (N_META, QB), 0)
    qm = lax.broadcasted_iota(jnp.int32, (N_META, QB), 1)
    for h in range(ATTN_HEADS):
        lanes = slice(h * QB, (h + 1) * QB)
        far = rb_ref[REL_BUCKETS - 1, h] * LOG2E
        tb_ref[0, :, lanes] = jnp.full((QB, QB), far, F32)
        tb_ref[1, :, lanes] = _bias_tile(q - k + QB, rb_ref, h) * LOG2E
        tb_ref[2, :, lanes] = _bias_tile(jnp.maximum(q - k, 0), rb_ref, h) * LOG2E
        tm_ref[0, :, lanes] = jnp.full((N_META, QB), far, F32)
        tm_ref[1, :, lanes] = _bias_tile(N_META + qm - km, rb_ref, h) * LOG2E


def _bias_tables(rel_bias):
    wide = ATTN_HEADS * QB
    return pl.pallas_call(
        _bias_kernel,
        in_specs=[pl.BlockSpec(memory_space=pltpu.SMEM)],
        out_shape=(jax.ShapeDtypeStruct((3, QB, wide), F32), jax.ShapeDtypeStruct((2, N_META, wide), F32)),
        name="bias_tables",
    )(rel_bias)


def _to_ukey(s):
    bits = pltpu.bitcast(s, jnp.int32)
    return bits ^ ((bits >> 31) | INT_MIN)


def _bit_transpose(x):
    x = list(x)
    j, m = 16, 0x0000FFFF
    while j:
        k = 0
        while k < WORD_BITS:
            t = (x[k] ^ lax.shift_right_logical(x[k + j], jnp.int32(j))) & m
            x[k] = x[k] ^ t
            x[k + j] = x[k + j] ^ jnp.left_shift(t, j)
            k = (k + j + 1) & ~j
        j >>= 1
        m = (m ^ (m << j)) & 0xFFFFFFFF
    return x


def _attn_kernel(k_top, qit_ref, kib_ref, wt_ref, qlt_ref, cb_ref, cbt_ref, cm_ref, cmt_ref, ga_ref,
                 tb_ref, tm_ref, wuvt_ref, a_ref,
                 s_scr, plane_scr, tie_scr, sel_scr, thr_scr, ngt_scr, am_scr, lg_scr, top_scr, mrun_scr, oacc_scr):
    i = pl.program_id(1)
    nch = i // KEY_TILES + 1
    wide = ATTN_HEADS * QB
    prows = plane_scr.shape[0] // WORD_BITS

    def rows_of(c):
        return pl.ds(pl.multiple_of(c * CK, CK), CK)

    def tile_rows(j):
        return pl.ds(pl.multiple_of(j * QB, QB), QB)

    def group_word(ref, g):
        return ref[pl.ds(pl.multiple_of(g * SUBLANES, SUBLANES), SUBLANES), :]

    qit = qit_ref[...]
    wt = wt_ref[...]
    k_loc = lax.broadcasted_iota(jnp.int32, (CK, QB), 0)
    q_pos = i * QB + lax.broadcasted_iota(jnp.int32, (CK, QB), 1)

    def score_chunk(c, last):
        kc = kib_ref[rows_of(c), :]
        s = jnp.maximum(_dot(kc, qit[0:IDX_DIM]), 0.0) * wt[0:1]
        for h in range(1, IDX_HEADS):
            s = s + jnp.maximum(_dot(kc, qit[h * IDX_DIM:(h + 1) * IDX_DIM]), 0.0) * wt[h:h + 1]
        if last:
            s = jnp.where(c * CK + k_loc <= q_pos, s, -jnp.inf)
        s_scr[rows_of(c), :] = s
        u = _to_ukey(s)
        for gl in range(CK // GROUP):
            words = [u[gl * GROUP + j * SUBLANES:gl * GROUP + (j + 1) * SUBLANES] for j in range(WORD_BITS)]
            g = c * (CK // GROUP) + gl
            for p, word in enumerate(_bit_transpose(words)):
                plane_scr[pl.ds(pl.multiple_of(p * prows + g * SUBLANES, SUBLANES), SUBLANES), :] = word

    def score_body(c, carry):
        score_chunk(c, False)
        return carry

    lax.fori_loop(0, nch - 1, score_body, 0)
    score_chunk(nch - 1, True)

    kf = float(k_top)
    group = lax.broadcasted_iota(jnp.int32, (prows, QB), 0) // SUBLANES
    active = jnp.where(group < nch * (CK // GROUP), -1, 0)

    def ones_per_query(x):
        return jnp.sum(_fold_rows(lax.population_count(x), jnp.add).astype(F32), axis=0, keepdims=True)

    def from_ukey(u):
        return pltpu.bitcast(jnp.where(u < 0, u ^ INT_MIN, ~u), F32)

    def bit_body(p, carry):
        alive, want, thr = carry
        hi = alive & plane_scr[pl.ds(pl.multiple_of(p * prows, prows), prows), :]
        n_hi = ones_per_query(hi)
        up = n_hi >= want
        return (jnp.where(up, hi, alive ^ hi), jnp.where(up, want, want - n_hi),
                thr | jnp.where(up, jnp.left_shift(jnp.int32(1), WORD_BITS - 1 - p), 0))

    _, _, thr_bits = lax.fori_loop(
        0, WORD_BITS, bit_body,
        (active, jnp.full((1, QB), kf, F32), jnp.zeros((1, QB), jnp.int32)))

    def census(thr_f):
        def body(c, acc):
            s = s_scr[rows_of(c), :]
            for gl in range(CK // GROUP):
                bits = [jnp.where(s[gl * GROUP + j * SUBLANES:gl * GROUP + (j + 1) * SUBLANES] == thr_f,
                                  _WORD_BIT[j], 0) for j in range(WORD_BITS)]
                tie_scr[pl.ds(pl.multiple_of((c * (CK // GROUP) + gl) * SUBLANES, SUBLANES), SUBLANES), :] = (
                    _fold_rows(jnp.concatenate(bits, axis=0), jnp.bitwise_or))
            return acc + jnp.sum(jnp.where(s > thr_f, 1.0, 0.0).reshape(KEY_TILES, QB, QB), axis=0)
        acc = lax.fori_loop(0, nch, body, jnp.zeros((QB, QB), F32))
        return jnp.sum(acc, axis=0, keepdims=True)

    def keep(thr_f, n_gt):
        thr_scr[...] = jnp.broadcast_to(thr_f, (SUBLANES, QB))
        ngt_scr[...] = jnp.broadcast_to(n_gt, (SUBLANES, QB))

    thr_fast = from_ukey(thr_bits)
    n_gt_fast = census(thr_fast)
    n_eq_fast = ones_per_query(tie_scr[...] & active)
    keep(thr_fast, n_gt_fast)
    unconfirmed = (n_gt_fast >= kf) | (n_gt_fast + n_eq_fast < kf)

    @pl.when(jnp.max(jnp.where(unconfirmed, 1.0, 0.0)) > 0.5)
    def _():
        def count_ge(cand_f):
            def body(c, acc):
                hit = jnp.where(s_scr[rows_of(c), :] >= cand_f, 1.0, 0.0)
                return acc + jnp.sum(hit.reshape(KEY_TILES, QB, QB), axis=0)
            acc = lax.fori_loop(0, nch, body, jnp.zeros((QB, QB), F32))
            return jnp.sum(acc, axis=0, keepdims=True)

        def slow_body(p, prefix):
            cand = prefix | jnp.left_shift(jnp.int32(1), WORD_BITS - 1 - p)
            floor = jnp.where(cand >= 0, jnp.maximum(cand, NEG_INF_KEY ^ INT_MIN), cand)
            return jnp.where(count_ge(from_ukey(floor)) >= kf, cand, prefix)

        thr_slow = from_ukey(lax.fori_loop(0, WORD_BITS, slow_body, jnp.zeros((1, QB), jnp.int32)))
        keep(thr_slow, census(thr_slow))

    thr_f = thr_scr[0:1, :]
    n_gt = ngt_scr[0:1, :]

    rest = jnp.where(thr_f == -jnp.inf, 0.0, kf - n_gt)
    alive = tie_scr[...] & active
    row = lax.broadcasted_iota(jnp.int32, (prows, QB), 0)
    group_bits = [1 << b for b in reversed(range((prows // SUBLANES - 1).bit_length()))]
    low_half = ([jnp.where((row & (SUBLANES * gb)) == 0, -1, 0) for gb in group_bits]
                + [int(np.uint32(m).astype(np.int32)) for m in _WORD_INDEX_LOW]
                + [jnp.where((row & rb) == 0, -1, 0) for rb in (4, 2, 1)])
    ties, taken = alive, jnp.zeros((prows, QB), jnp.int32)
    for low in low_half:
        lo = ties & low
        n_lo = ones_per_query(lo)
        inside = n_lo >= rest
        taken = taken | jnp.where(inside, 0, lo)
        ties = jnp.where(inside, lo, ties ^ lo)
        rest = jnp.where(inside, rest, rest - n_lo)
    sel_scr[...] = taken | jnp.where(rest >= 1.0, ties, 0)

    def mask_body(c, carry):
        s = s_scr[rows_of(c), :]
        for gl in range(CK // GROUP):
            g = c * (CK // GROUP) + gl
            word = group_word(sel_scr, g)
            am_scr[pl.ds(pl.multiple_of(g * GROUP, GROUP), GROUP), :] = jnp.concatenate(
                [jnp.where(s[gl * GROUP + j * SUBLANES:gl * GROUP + (j + 1) * SUBLANES] > thr_f, 0.0,
                           jnp.where(jnp.left_shift(word, j) < 0, 0.0, -jnp.inf))
                 for j in range(WORD_BITS)], axis=0)
        return carry

    lax.fori_loop(0, nch, mask_body, 0)

    qt = jnp.concatenate([qlt_ref[h] for h in range(ATTN_HEADS)], axis=1)

    def make_logits(slot, c):
        lg = _dot(cb_ref[rows_of(c), :], qt)
        top = jnp.full((SUBLANES, wide), -jnp.inf, F32)
        for t in range(KEY_TILES):
            j = c * KEY_TILES + t
            kind = jnp.where(j == i, 2, jnp.where(j == i - 1, 1, 0))
            am = am_scr[tile_rows(j), :]
            lgt = lg[t * QB:(t + 1) * QB] + tb_ref[kind] + jnp.concatenate([am] * ATTN_HEADS, axis=1)
            lg_scr[slot, t * QB:(t + 1) * QB, :] = lgt
            top = jnp.maximum(top, _fold_rows(lgt, jnp.maximum))
        top_scr[slot] = top

    def absorb(slot, c):
        m_run = mrun_scr[0:1, :]
        m_new = jnp.maximum(m_run, jnp.max(top_scr[slot], axis=0, keepdims=True))
        pr = jnp.exp2(lg_scr[slot] - m_new)
        ct = cbt_ref[pl.ds(c * KEY_TILES, KEY_TILES)]
        ct = jnp.concatenate([ct[t] for t in range(KEY_TILES)], axis=1)
        oacc_scr[...] = oacc_scr[...] * jnp.exp2(m_run - m_new) + _dot(ct, pr.astype(BF16))
        mrun_scr[...] = jnp.broadcast_to(m_new, (SUBLANES, wide))

    lgm = _dot(cm_ref[...], qt) + tm_ref[jnp.where(i == 0, 1, 0)]
    m_meta = jnp.max(lgm, axis=0, keepdims=True)
    mrun_scr[...] = jnp.broadcast_to(m_meta, (SUBLANES, wide))
    oacc_scr[...] = _dot(cmt_ref[...], jnp.exp2(lgm - m_meta).astype(BF16))

    make_logits(0, 0)

    def pair_body(p, carry):
        make_logits(1, 2 * p + 1)
        absorb(0, 2 * p)
        make_logits(0, 2 * p + 2)
        absorb(1, 2 * p + 1)
        return carry

    pairs = (nch - 1) // 2
    lax.fori_loop(0, pairs, pair_body, 0)
    rest_two = nch - 2 * pairs == 2

    @pl.when(rest_two)
    def _():
        make_logits(1, nch - 1)

    absorb(0, 2 * pairs)

    @pl.when(rest_two)
    def _():
        absorb(1, nch - 1)

    o = (oacc_scr[:KV_RANK, :] / oacc_scr[KV_RANK:KV_RANK + 1, :]).astype(BF16)
    o_all = jnp.concatenate([o[:, h * QB:(h + 1) * QB] for h in range(ATTN_HEADS)], axis=0)
    a_t = _dot(wuvt_ref[...], o_all)
    a_ref[...] = a_t.T * ga_ref[...]


def _attention(k_top, batch, seq, qit, kib, wt, qlt, cb, cbt, cm, cmt, ga, tb, tm, wuvt):
    nq = seq // QB
    full = lambda a: pl.BlockSpec(a.shape, lambda b, i: (0,) * a.ndim)
    qcols = lambda n: pl.BlockSpec((n, QB), lambda b, i: (0, b * nq + i))
    per_batch = lambda w: pl.BlockSpec((seq, w), lambda b, i: (b, 0))
    return pl.pallas_call(
        functools.partial(_attn_kernel, k_top),
        grid=(batch, nq),
        in_specs=[
            qcols(IDX_HEADS * IDX_DIM),
            per_batch(IDX_DIM),
            qcols(SUBLANES),
            pl.BlockSpec((ATTN_HEADS, KV_RANK, QB), lambda b, i: (0, 0, b * nq + i)),
            per_batch(KV_RANK),
            pl.BlockSpec((nq, PV_ROWS, QB), lambda b, i: (b, 0, 0)),
            full(cm), full(cmt),
            pl.BlockSpec((QB, ATTN_WIDTH), lambda b, i: (b * nq + i, 0)),
            full(tb), full(tm), full(wuvt),
        ],
        out_specs=pl.BlockSpec((QB, ATTN_WIDTH), lambda b, i: (b * nq + i, 0)),
        out_shape=jax.ShapeDtypeStruct((batch * seq, ATTN_WIDTH), F32),
        scratch_shapes=[
            pltpu.VMEM((seq, QB), F32),
            pltpu.VMEM((seq, QB), jnp.int32),
            pltpu.VMEM((seq // WORD_BITS, QB), jnp.int32),
            pltpu.VMEM((seq // WORD_BITS, QB), jnp.int32),
            pltpu.VMEM((SUBLANES, QB), F32),
            pltpu.VMEM((SUBLANES, QB), F32),
            pltpu.VMEM((seq, QB), F32),
            pltpu.VMEM((2, CK, ATTN_HEADS * QB), F32),
            pltpu.VMEM((2, SUBLANES, ATTN_HEADS * QB), F32),
            pltpu.VMEM((SUBLANES, ATTN_HEADS * QB), F32),
            pltpu.VMEM((PV_ROWS, ATTN_HEADS * QB), F32),
        ],
        compiler_params=_params("arbitrary", "arbitrary"),
        name="dsa_attention",
    )(qit, kib, wt, qlt, cb, cbt, cm, cmt, ga, tb, tm, wuvt)


def _meta_attn_kernel(qlt_ref, cm_ref, cmt_ref, ga_ref, tb_ref, wuvt_ref, a_ref, o_scr):
    cm = cm_ref[...]
    cmt = cmt_ref[...]
    k = lax.broadcasted_iota(jnp.int32, (N_META, N_META), 0)
    q = lax.broadcasted_iota(jnp.int32, (N_META, N_META), 1)
    for h in range(ATTN_HEADS):
        lg = _dot(cm, qlt_ref[h]) + tb_ref[2, :N_META, h * QB:h * QB + N_META]
        lg = jnp.where(k <= q, lg, -jnp.inf)
        pr = jnp.exp2(lg - jnp.max(lg, axis=0, keepdims=True))
        pr = pr / jnp.sum(pr, axis=0, keepdims=True)
        o_scr[h * KV_RANK:(h + 1) * KV_RANK, :] = _dot(cmt, pr.astype(BF16)).astype(BF16)
    a_ref[...] = _dot(wuvt_ref[...], o_scr[...]).T * ga_ref[...]


def _meta_attention(qlt_m, cm, cmt, ga_m, tb, wuvt):
    return pl.pallas_call(
        _meta_attn_kernel,
        out_shape=jax.ShapeDtypeStruct((N_META, ATTN_WIDTH), F32),
        scratch_shapes=[pltpu.VMEM((ATTN_HEADS * KV_RANK, N_META), BF16)],
        name="meta_attention",
    )(qlt_m, cm, cmt, ga_m, tb, wuvt)


def _hgrn_levels(chunk):
    return [1 << p for p in range(int(math.log2(chunk)))]


def _hgrn_masks(chunk):
    t = np.arange(chunk)[:, None]
    s = np.arange(chunk)[None, :]
    masks = [t == s]
    for m in _hgrn_levels(chunk):
        same = (t // (2 * m)) == (s // (2 * m))
        masks.append(same & ((t & m) != 0) & ((s & m) == 0))
    return np.stack(masks).astype(np.float32)


def _block_ref_rows(b, m, chunk, row):
    if 2 * m >= SUBLANES:
        pieces = []
        for blk in range(chunk // (2 * m)):
            r = blk * 2 * m + m - 1
            pieces.append(jnp.broadcast_to(b[r:r + 1, :], (2 * m, b.shape[1])))
        return pieces[0] if len(pieces) == 1 else jnp.concatenate(pieces, axis=0)
    delta = (row & (2 * m - 1)) - (m - 1)
    out = b
    for d in range(-(m - 1), m + 1):
        if d != 0:
            out = jnp.where(delta == d, pltpu.roll(b, d % chunk, 0), out)
    return out


def _split3(x):
    hi = x.astype(BF16)
    r1 = x - hi.astype(F32)
    mid = r1.astype(BF16)
    lo = (r1 - mid.astype(F32)).astype(BF16)
    return hi, mid, lo


def _hgrn_kernel(chunk, nsub, qf_ref, g_ref, kk_ref, v_ref, gh_ref, s0_ref, ng_ref, tri_ref, lm_ref,
                 r_ref, sfin_ref, st_scr):
    ci = pl.program_id(1)

    @pl.when(ci == 0)
    def _():
        st_scr[...] = s0_ref[...]

    tri = tri_ref[...]
    row = lax.broadcasted_iota(jnp.int32, (chunk, HGRN_EXPAND), 0)
    levels = _hgrn_levels(chunk)
    for hd, sub in [(hd, sub) for hd in range(HGRN_HEADS) for sub in range(nsub)]:
        sl = (slice(sub * chunk, (sub + 1) * chunk), slice(hd * HGRN_EXPAND, (hd + 1) * HGRN_EXPAND))
        q = qf_ref[sl]
        g = g_ref[sl]
        k = kk_ref[sl]
        vb = v_ref[sl].astype(BF16)
        g_hi, g_mid, g_lo = _split3(g)
        b = _dot(tri, g_hi) + _dot(tri, g_mid) + _dot(tri, g_lo)

        a = _dot_nt(q.astype(BF16), k.astype(BF16)) * lm_ref[0]
        for li, m in enumerate(levels):
            bref = _block_ref_rows(b, m, chunk, row)
            e = jnp.exp(jnp.where((row & m) != 0, b - bref, bref - b))
            a = a + _dot_nt((q * e).astype(BF16), (k * e).astype(BF16)) * lm_ref[li + 1]

        st = st_scr[hd]
        o = _dot_nt((q * jnp.exp(b)).astype(BF16), st.astype(BF16)) + _dot(a.astype(BF16), vb)
        b_last = b[chunk - 1:chunk, :]
        kd = (k * jnp.exp(b_last - b)).astype(BF16)
        st_scr[hd] = st * jnp.exp(b_last) + _dot_tn(vb, kd)

        rn = o * lax.rsqrt(jnp.mean(o * o, axis=-1, keepdims=True) + EPS) * ng_ref[...]
        r_ref[sl] = rn * gh_ref[sl]

    @pl.when(ci == pl.num_programs(1) - 1)
    def _():
        sfin_ref[0] = st_scr[...]


def _hgrn(batch, seq, chunk, nsub, qf, g, kk, v, gh, s0, ng):
    nc = seq // (chunk * nsub)
    tri = jnp.asarray(np.tril(np.ones((chunk, chunk), np.float32)), BF16)
    lm = jnp.asarray(_hgrn_masks(chunk))
    blk = pl.BlockSpec((chunk * nsub, HGRN_WIDTH), lambda b, c: (b * nc + c, 0))
    full = lambda a: pl.BlockSpec(a.shape, lambda b, c: (0,) * a.ndim)
    return pl.pallas_call(
        functools.partial(_hgrn_kernel, chunk, nsub),
        grid=(batch, nc),
        in_specs=[blk, blk, blk, blk, blk, full(s0), full(ng), full(tri), full(lm)],
        out_specs=(blk, pl.BlockSpec((1, HGRN_HEADS, HGRN_EXPAND, HGRN_EXPAND),
                                     lambda b, c: (b, 0, 0, 0))),
        out_shape=(jax.ShapeDtypeStruct((batch * seq, HGRN_WIDTH), F32),
                   jax.ShapeDtypeStruct((batch, HGRN_HEADS, HGRN_EXPAND, HGRN_EXPAND), F32)),
        scratch_shapes=[pltpu.VMEM((HGRN_HEADS, HGRN_EXPAND, HGRN_EXPAND), F32)],
        compiler_params=_params("arbitrary", "arbitrary"),
        name=f"hgrn2_c{chunk}x{nsub}",
    )(qf, g, kk, v, gh, s0, ng, tri, lm)


def _out_kernel(a_ref, r_ref, h_ref, wo_ref, lg_ref, lb_ref, o_ref):
    y = (_dot(a_ref[...].astype(BF16), wo_ref[:ATTN_WIDTH, :])
         + _dot(r_ref[...].astype(BF16), wo_ref[ATTN_WIDTH:, :]))
    z = DN_ALPHA * h_ref[...] + y
    mu = jnp.mean(z, axis=-1, keepdims=True)
    zc = z - mu
    var = jnp.mean(zc * zc, axis=-1, keepdims=True)
    o_ref[...] = zc * lax.rsqrt(var + EPS) * lg_ref[...] + lb_ref[...]


def _out_project(a, r, h2d, wo, lng, lnb, rows):
    t = h2d.shape[0]
    row_spec = lambda w: pl.BlockSpec((rows, w), lambda i: (i, 0))
    full = lambda x: pl.BlockSpec(x.shape, lambda i: (0,) * x.ndim)
    return pl.pallas_call(
        _out_kernel,
        grid=(t // rows,),
        in_specs=[row_spec(ATTN_WIDTH), row_spec(HGRN_WIDTH), row_spec(D_MODEL),
                  full(wo), full(lng), full(lnb)],
        out_specs=row_spec(D_MODEL),
        out_shape=jax.ShapeDtypeStruct((t, D_MODEL), F32),
        compiler_params=_params("arbitrary"),
        name=f"out_proj_r{rows}",
    )(a, r, h2d, wo, lng, lnb)


def _block_diag(blocks):
    n, r, c = blocks.shape
    eye = jnp.eye(n, dtype=blocks.dtype)
    return (eye[:, None, :, None] * blocks[:, :, None, :]).reshape(n * r, n * c)


def kernel(x, meta_tokens, rel_bias, hgrn_lb_raw, w_in, kv_norm_g, w_uk, w_uv, hgrn_norm_g, w_out, ln_g, ln_b):
    batch, seq, _ = x.shape
    assert seq % CK == 0 and seq % (HGRN_CHUNK * HGRN_STEP_CHUNKS) == 0 and (batch * seq) % OUT_ROWS == 0
    assert seq // GROUP < WORD_BITS
    k_top = min(TOPK_MAX, seq // 4)

    rel_bias = rel_bias.astype(F32)
    tb, tm = _bias_tables(rel_bias)
    lbraw = hgrn_lb_raw.astype(F32)
    h = x.reshape(batch * seq, D_MODEL).astype(F32)
    hm = meta_tokens.astype(F32)
    s_zero = jnp.zeros((HGRN_HEADS, HGRN_EXPAND, HGRN_EXPAND), F32)

    wp_all = jnp.concatenate(
        [w_in[:, :, :_RAW_SPLIT], jnp.zeros((DEPTH, D_MODEL, _P_GA - _RAW_SPLIT), w_in.dtype),
         w_in[:, :, _RAW_SPLIT:]], axis=2).astype(BF16)

    for l in range(DEPTH):
        wp = wp_all[l]
        wuk_bd = _block_diag(w_uk[l]).astype(BF16)
        wuvt = _block_diag(w_uv[l]).T.astype(BF16)
        wo = w_out[l].astype(BF16)
        kvg = kv_norm_g[l].reshape(1, KV_RANK).astype(F32)
        ng = hgrn_norm_g[l].reshape(1, HGRN_EXPAND).astype(F32)
        lng = ln_g[l].reshape(1, D_MODEL).astype(F32)
        lnb = ln_b[l].reshape(1, D_MODEL).astype(F32)

        hm_pad = jnp.pad(hm, ((0, QB - N_META), (0, 0)))
        (qlt_m, cb_m, cbt_m, _, _, _, ga_m, qf_m, g_m, kk_m, v_m, gh_m) = _project(
            l, hm_pad, wp, wuk_bd, kvg, lbraw, QB)
        cm = cb_m[:N_META]
        cmt = cbt_m[0][:, :N_META]
        a_m = _meta_attention(qlt_m[:, :, :N_META], cm, cmt[:KV_RANK], ga_m[:N_META], tb, wuvt)
        r_m, s_m = _hgrn(1, N_META, N_META, 1, qf_m[:N_META], g_m[:N_META], kk_m[:N_META],
                         v_m[:N_META], gh_m[:N_META], s_zero, ng)

        (qlt, cb, cbt, qit, kib, wt, ga, qf, g, kk, v, gh) = _project(
            l, h, wp, wuk_bd, kvg, lbraw, PROJ_ROWS)
        a = _attention(k_top, batch, seq, qit, kib, wt, qlt, cb, cbt, cm, cmt, ga, tb, tm, wuvt)
        r, _ = _hgrn(batch, seq, HGRN_CHUNK, HGRN_STEP_CHUNKS, qf, g, kk, v, gh, s_m[0], ng)

        hm = _out_project(a_m, r_m, hm, wo, lng, lnb, N_META)
        h = _out_project(a, r, h, wo, lng, lnb, OUT_ROWS)

    return h.reshape(batch, seq, D_MODEL).astype(x.dtype)
```

```python
import functools
import math

import numpy as np
import jax
import jax.numpy as jnp
from jax import lax
from jax.experimental import pallas as pl
from jax.experimental.pallas import tpu as pltpu

D_MODEL = 1024
DEPTH = 2
N_META = 16
ATTN_WIDTH = 512
HGRN_WIDTH = 512
ATTN_HEADS = 8
ATTN_HEAD_DIM = 64
KV_RANK = 128
IDX_HEADS = 4
IDX_DIM = 64
TOPK_MAX = 256
HGRN_EXPAND = 128
HGRN_HEADS = 4
REL_BUCKETS = 32
REL_MAX_DIST = 128
DN_ALPHA = (2 * DEPTH) ** 0.25
EPS = 1e-6
LOG2E = math.log2(math.e)

F32 = jnp.float32
BF16 = jnp.bfloat16

LANES = 128
SUBLANES = 8
QB = 128
KEY_TILES = 4
CK = KEY_TILES * QB
WORD_BITS = 32
GROUP = WORD_BITS * SUBLANES
PV_ROWS = KV_RANK + 16
HGRN_CHUNK = 128
HGRN_STEP_CHUNKS = 4
PROJ_ROWS = 512
OUT_ROWS = 1024
VMEM_LIMIT = 56 * 1024 * 1024

_P_QA, _P_CKV, _P_QI, _P_KW, _P_GA, _P_QH, _P_FH, _P_IH, _P_GH, _P_END = (
    0, 512, 640, 896, 1024, 1536, 2048, 2560, 3072, 3584)
_RAW_SPLIT = 964

NEG_INF_KEY = -2139095041
_WORD_INDEX_LOW = (0xFFFF0000, 0xFF00FF00, 0xF0F0F0F0, 0xCCCCCCCC, 0xAAAAAAAA)
_WORD_BIT = [int(np.uint32(1 << (31 - j)).astype(np.int32)) for j in range(32)]
INT_MIN = -2147483648


def _bucket_starts():
    max_exact = REL_BUCKETS // 2
    n = np.arange(0, 4 * REL_MAX_DIST)
    nf = np.maximum(n, 1).astype(np.float64)
    large = max_exact + (np.log(nf / max_exact) / math.log(REL_MAX_DIST / max_exact)
                         * (REL_BUCKETS - max_exact)).astype(np.int64)
    large = np.minimum(large, REL_BUCKETS - 1)
    bucket = np.where(n < max_exact, n, large)
    return [int(np.argmax(bucket >= b)) for b in range(REL_BUCKETS)]


_BUCKET_START = _bucket_starts()


def _dot(a, b):
    return jnp.dot(a, b, preferred_element_type=F32)


def _dot_nt(a, b):
    return lax.dot_general(a, b, (((1,), (1,)), ((), ())), preferred_element_type=F32)


def _dot_tn(a, b):
    return lax.dot_general(a, b, (((0,), (0,)), ((), ())), preferred_element_type=F32)


def _sigmoid(x):
    return 1.0 / (1.0 + jnp.exp(-x))


def _params(*sem):
    return pltpu.CompilerParams(dimension_semantics=sem, vmem_limit_bytes=VMEM_LIMIT)


def _fold_rows(x, op):
    parts = [x[r:r + SUBLANES] for r in range(0, x.shape[0], SUBLANES)]
    while len(parts) > 1:
        pairs = [op(parts[k], parts[k + 1]) for k in range(0, len(parts) - 1, 2)]
        parts = pairs + parts[len(parts) - len(parts) % 2:]
    return parts[0]


def _proj_kernel(layer, h_ref, wp_ref, wuk_ref, kvg_ref, lbraw_ref,
                 qlt_ref, cb_ref, cbt_ref, qit_ref, kib_ref, wt_ref, ga_ref,
                 qf_ref, g_ref, kk_ref, v_ref, gh_ref):
    rows = h_ref.shape[0]
    hb = h_ref[...].astype(BF16)

    def proj(lo, hi):
        return _dot(hb, wp_ref[:, lo:hi])

    ql = _dot(proj(_P_QA, _P_CKV).astype(BF16), wuk_ref[...]) * (ATTN_HEAD_DIM ** -0.5 * LOG2E)
    qlt_ref[...] = ql.T.reshape(ATTN_HEADS, KV_RANK, rows).astype(BF16)

    ckv = proj(_P_CKV, _P_QI)
    c = ckv * lax.rsqrt(jnp.mean(ckv * ckv, axis=-1, keepdims=True) + EPS) * kvg_ref[...]
    cb_ref[...] = c.astype(BF16)
    ct = c.T
    ones_row = jnp.where(lax.broadcasted_iota(jnp.int32, (PV_ROWS - KV_RANK, QB), 0) == 0, 1.0, 0.0)
    for t in range(rows // QB):
        cbt_ref[t] = jnp.concatenate([ct[:, t * QB:(t + 1) * QB], ones_row], axis=0).astype(BF16)

    qit_ref[...] = proj(_P_QI, _P_KW).T.astype(BF16)
    kw = proj(_P_KW, _P_GA)
    kib_ref[...] = kw[:, :IDX_DIM].astype(BF16)
    wt_ref[...] = kw.T[IDX_DIM:IDX_DIM + SUBLANES, :] * (IDX_HEADS ** -0.5 * IDX_DIM ** -0.5)

    ga = proj(_P_GA, _P_QH)
    ga_ref[...] = ga * _sigmoid(ga)

    qh = proj(_P_QH, _P_FH)
    qf_ref[...] = qh * _sigmoid(qh) * (HGRN_EXPAND ** -0.5)

    raw = lbraw_ref[...]
    ex = jnp.exp(raw - jnp.max(raw, axis=0, keepdims=True))
    lbp = ex / jnp.sum(ex, axis=0, keepdims=True)
    lb = lbp[0:1]
    for i in range(1, layer + 1):
        lb = lb + lbp[i:i + 1]
    lb = lb - lbp[0:1]
    f = lb + (1.0 - lb) * _sigmoid(proj(_P_FH, _P_IH))
    g_ref[...] = jnp.log2(f)
    kk_ref[...] = 1.0 - f
    v_ref[...] = proj(_P_IH, _P_GH)
    gh = proj(_P_GH, _P_END)
    gh_ref[...] = gh * _sigmoid(gh)


def _project(layer, h2d, wp, wuk_bd, kvg, lbraw, rows):
    t = h2d.shape[0]
    grid = (t // rows,)
    row_spec = lambda w: pl.BlockSpec((rows, w), lambda r: (r, 0))
    col_spec = lambda n: pl.BlockSpec((n, rows), lambda r: (0, r))
    full = lambda a: pl.BlockSpec(a.shape, lambda r: (0,) * a.ndim)
    out_shape = (
        jax.ShapeDtypeStruct((ATTN_HEADS, KV_RANK, t), BF16),
        jax.ShapeDtypeStruct((t, KV_RANK), BF16),
        jax.ShapeDtypeStruct((t // QB, PV_ROWS, QB), BF16),
        jax.ShapeDtypeStruct((IDX_HEADS * IDX_DIM, t), BF16),
        jax.ShapeDtypeStruct((t, IDX_DIM), BF16),
        jax.ShapeDtypeStruct((SUBLANES, t), F32),
        jax.ShapeDtypeStruct((t, ATTN_WIDTH), F32),
        jax.ShapeDtypeStruct((t, HGRN_WIDTH), F32),
        jax.ShapeDtypeStruct((t, HGRN_WIDTH), F32),
        jax.ShapeDtypeStruct((t, HGRN_WIDTH), F32),
        jax.ShapeDtypeStruct((t, HGRN_WIDTH), F32),
        jax.ShapeDtypeStruct((t, HGRN_WIDTH), F32),
    )
    out_specs = (
        pl.BlockSpec((ATTN_HEADS, KV_RANK, rows), lambda r: (0, 0, r)),
        row_spec(KV_RANK),
        pl.BlockSpec((rows // QB, PV_ROWS, QB), lambda r: (r, 0, 0)),
        col_spec(IDX_HEADS * IDX_DIM), row_spec(IDX_DIM), col_spec(SUBLANES),
        row_spec(ATTN_WIDTH), row_spec(HGRN_WIDTH), row_spec(HGRN_WIDTH), row_spec(HGRN_WIDTH),
        row_spec(HGRN_WIDTH), row_spec(HGRN_WIDTH),
    )
    return pl.pallas_call(
        functools.partial(_proj_kernel, layer),
        grid=grid,
        in_specs=[row_spec(D_MODEL), full(wp), full(wuk_bd), full(kvg), full(lbraw)],
        out_specs=out_specs,
        out_shape=out_shape,
        compiler_params=_params("arbitrary"),
        name=f"proj_l{layer}_r{rows}",
    )(h2d, wp, wuk_bd, kvg, lbraw)


def _bias_tile(dist, rb_ref, h):
    out = jnp.full(dist.shape, rb_ref[0, h], F32)
    for b in range(1, REL_BUCKETS):
        out = jnp.where(dist >= _BUCKET_START[b], rb_ref[b, h], out)
    return out


def _bias_kernel(rb_ref, tb_ref, tm_ref):
    k = lax.broadcasted_iota(jnp.int32, (QB, QB), 0)
    q = lax.broadcasted_iota(jnp.int32, (QB, QB), 1)
    km = lax.broadcasted_iota(jnp.int32, (N_META, QB), 0)
    qm = lax.broadcasted_iota(jnp.int32, (N_META, QB), 1)
    for h in range(ATTN_HEADS):
        lanes = slice(h * QB, (h + 1) * QB)
        far = rb_ref[REL_BUCKETS - 1, h] * LOG2E
        tb_ref[0, :, lanes] = jnp.full((QB, QB), far, F32)
        tb_ref[1, :, lanes] = _bias_tile(q - k + QB, rb_ref, h) * LOG2E
        tb_ref[2, :, lanes] = _bias_tile(jnp.maximum(q - k, 0), rb_ref, h) * LOG2E
        tm_ref[0, :, lanes] = jnp.full((N_META, QB), far, F32)
        tm_ref[1, :, lanes] = _bias_tile(N_META + qm - km, rb_ref, h) * LOG2E


def _bias_tables(rel_bias):
    wide = ATTN_HEADS * QB
    return pl.pallas_call(
        _bias_kernel,
        in_specs=[pl.BlockSpec(memory_space=pltpu.SMEM)],
        out_shape=(jax.ShapeDtypeStruct((3, QB, wide), F32), jax.ShapeDtypeStruct((2, N_META, wide), F32)),
        name="bias_tables",
    )(rel_bias)


def _to_ukey(s):
    bits = pltpu.bitcast(s, jnp.int32)
    return bits ^ ((bits >> 31) | INT_MIN)


def _bit_transpose(x):
    x = list(x)
    j, m = 16, 0x0000FFFF
    while j:
        k = 0
        while k < WORD_BITS:
            t = (x[k] ^ lax.shift_right_logical(x[k + j], jnp.int32(j))) & m
            x[k] = x[k] ^ t
            x[k + j] = x[k + j] ^ jnp.left_shift(t, j)
            k = (k + j + 1) & ~j
        j >>= 1
        m = (m ^ (m << j)) & 0xFFFFFFFF
    return x


def _attn_kernel(k_top, qit_ref, kib_ref, wt_ref, qlt_ref, cb_ref, cbt_ref, cm_ref, cmt_ref, ga_ref,
                 tb_ref, tm_ref, wuvt_ref, a_ref,
                 s_scr, plane_scr, tie_scr, sel_scr, thr_scr, ngt_scr, am_scr, lg_scr, top_scr, mrun_scr, oacc_scr):
    i = pl.program_id(1)
    nch = i // KEY_TILES + 1
    wide = ATTN_HEADS * QB
    prows = plane_scr.shape[0] // WORD_BITS

    def rows_of(c):
        return pl.ds(pl.multiple_of(c * CK, CK), CK)

    def tile_rows(j):
        return pl.ds(pl.multiple_of(j * QB, QB), QB)

    def group_word(ref, g):
        return ref[pl.ds(pl.multiple_of(g * SUBLANES, SUBLANES), SUBLANES), :]

    qit = qit_ref[...]
    wt = wt_ref[...]
    k_loc = lax.broadcasted_iota(jnp.int32, (CK, QB), 0)
    q_pos = i * QB + lax.broadcasted_iota(jnp.int32, (CK, QB), 1)

    def score_chunk(c, last):
        kc = kib_ref[rows_of(c), :]
        s = jnp.maximum(_dot(kc, qit[0:IDX_DIM]), 0.0) * wt[0:1]
        for h in range(1, IDX_HEADS):
            s = s + jnp.maximum(_dot(kc, qit[h * IDX_DIM:(h + 1) * IDX_DIM]), 0.0) * wt[h:h + 1]
        if last:
            s = jnp.where(c * CK + k_loc <= q_pos, s, -jnp.inf)
        s_scr[rows_of(c), :] = s
        u = _to_ukey(s)
        for gl in range(CK // GROUP):
            words = [u[gl * GROUP + j * SUBLANES:gl * GROUP + (j + 1) * SUBLANES] for j in range(WORD_BITS)]
            g = c * (CK // GROUP) + gl
            for p, word in enumerate(_bit_transpose(words)):
                plane_scr[pl.ds(pl.multiple_of(p * prows + g * SUBLANES, SUBLANES), SUBLANES), :] = word

    def score_body(p, carry):
        score_chunk(2 * p, False)
        score_chunk(2 * p + 1, False)
        return carry

    lax.fori_loop(0, (nch - 1) // 2, score_body, 0)

    @pl.when((nch - 1) % 2 == 1)
    def _():
        score_chunk(nch - 2, False)

    score_chunk(nch - 1, True)

    kf = float(k_top)
    group = lax.broadcasted_iota(jnp.int32, (prows, QB), 0) // SUBLANES
    active = jnp.where(group < nch * (CK // GROUP), -1, 0)

    def ones_per_query(x):
        return jnp.sum(_fold_rows(lax.population_count(x), jnp.add).astype(F32), axis=0, keepdims=True)

    def from_ukey(u):
        return pltpu.bitcast(jnp.where(u < 0, u ^ INT_MIN, ~u), F32)

    def bit_body(p, carry):
        alive, want, thr = carry
        hi = alive & plane_scr[pl.ds(pl.multiple_of(p * prows, prows), prows), :]
        n_hi = ones_per_query(hi)
        up = n_hi >= want
        return (jnp.where(up, hi, alive ^ hi), jnp.where(up, want, want - n_hi),
                thr | jnp.where(up, jnp.left_shift(jnp.int32(1), WORD_BITS - 1 - p), 0))

    _, _, thr_bits = lax.fori_loop(
        0, WORD_BITS, bit_body,
        (active, jnp.full((1, QB), kf, F32), jnp.zeros((1, QB), jnp.int32)))

    def census(thr_f):
        def body(c, acc):
            s = s_scr[rows_of(c), :]
            for gl in range(CK // GROUP):
                bits = [jnp.where(s[gl * GROUP + j * SUBLANES:gl * GROUP + (j + 1) * SUBLANES] == thr_f,
                                  _WORD_BIT[j], 0) for j in range(WORD_BITS)]
                tie_scr[pl.ds(pl.multiple_of((c * (CK // GROUP) + gl) * SUBLANES, SUBLANES), SUBLANES), :] = (
                    _fold_rows(jnp.concatenate(bits, axis=0), jnp.bitwise_or))
            return acc + jnp.sum(jnp.where(s > thr_f, 1.0, 0.0).reshape(KEY_TILES, QB, QB), axis=0)
        acc = lax.fori_loop(0, nch, body, jnp.zeros((QB, QB), F32))
        return jnp.sum(acc, axis=0, keepdims=True)

    def keep(thr_f, n_gt):
        thr_scr[...] = jnp.broadcast_to(thr_f, (SUBLANES, QB))
        ngt_scr[...] = jnp.broadcast_to(n_gt, (SUBLANES, QB))

    thr_fast = from_ukey(thr_bits)
    n_gt_fast = census(thr_fast)
    n_eq_fast = ones_per_query(tie_scr[...] & active)
    keep(thr_fast, n_gt_fast)
    unconfirmed = (n_gt_fast >= kf) | (n_gt_fast + n_eq_fast < kf)

    @pl.when(jnp.max(jnp.where(unconfirmed, 1.0, 0.0)) > 0.5)
    def _():
        def count_ge(cand_f):
            def body(c, acc):
                hit = jnp.where(s_scr[rows_of(c), :] >= cand_f, 1.0, 0.0)
                return acc + jnp.sum(hit.reshape(KEY_TILES, QB, QB), axis=0)
            acc = lax.fori_loop(0, nch, body, jnp.zeros((QB, QB), F32))
            return jnp.sum(acc, axis=0, keepdims=True)

        def slow_body(p, prefix):
            cand = prefix | jnp.left_shift(jnp.int32(1), WORD_BITS - 1 - p)
            floor = jnp.where(cand >= 0, jnp.maximum(cand, NEG_INF_KEY ^ INT_MIN), cand)
            return jnp.where(count_ge(from_ukey(floor)) >= kf, cand, prefix)

        thr_slow = from_ukey(lax.fori_loop(0, WORD_BITS, slow_body, jnp.zeros((1, QB), jnp.int32)))
        keep(thr_slow, census(thr_slow))

    thr_f = thr_scr[0:1, :]
    n_gt = ngt_scr[0:1, :]

    rest = jnp.where(thr_f == -jnp.inf, 0.0, kf - n_gt)
    alive = tie_scr[...] & active
    row = lax.broadcasted_iota(jnp.int32, (prows, QB), 0)
    group_bits = [1 << b for b in reversed(range((prows // SUBLANES - 1).bit_length()))]
    low_half = ([jnp.where((row & (SUBLANES * gb)) == 0, -1, 0) for gb in group_bits]
                + [int(np.uint32(m).astype(np.int32)) for m in _WORD_INDEX_LOW]
                + [jnp.where((row & rb) == 0, -1, 0) for rb in (4, 2, 1)])
    ties, taken = alive, jnp.zeros((prows, QB), jnp.int32)
    for low in low_half:
        lo = ties & low
        n_lo = ones_per_query(lo)
        inside = n_lo >= rest
        taken = taken | jnp.where(inside, 0, lo)
        ties = jnp.where(inside, lo, ties ^ lo)
        rest = jnp.where(inside, rest, rest - n_lo)
    sel_scr[...] = taken | jnp.where(rest >= 1.0, ties, 0)

    def mask_body(c, carry):
        s = s_scr[rows_of(c), :]
        for gl in range(CK // GROUP):
            g = c * (CK // GROUP) + gl
            word = group_word(sel_scr, g)
            am_scr[pl.ds(pl.multiple_of(g * GROUP, GROUP), GROUP), :] = jnp.concatenate(
                [jnp.where(s[gl * GROUP + j * SUBLANES:gl * GROUP + (j + 1) * SUBLANES] > thr_f, 0.0,
                           jnp.where(jnp.left_shift(word, j) < 0, 0.0, -jnp.inf))
                 for j in range(WORD_BITS)], axis=0)
        return carry

    lax.fori_loop(0, nch, mask_body, 0)

    qt = jnp.concatenate([qlt_ref[h] for h in range(ATTN_HEADS)], axis=1)

    def make_logits(slot, c):
        lg = _dot(cb_ref[rows_of(c), :], qt)
        top = jnp.full((SUBLANES, wide), -jnp.inf, F32)
        for t in range(KEY_TILES):
            j = c * KEY_TILES + t
            kind = jnp.where(j == i, 2, jnp.where(j == i - 1, 1, 0))
            am = am_scr[tile_rows(j), :]
            lgt = lg[t * QB:(t + 1) * QB] + tb_ref[kind] + jnp.concatenate([am] * ATTN_HEADS, axis=1)
            lg_scr[slot, t * QB:(t + 1) * QB, :] = lgt
            top = jnp.maximum(top, _fold_rows(lgt, jnp.maximum))
        top_scr[slot] = top

    def absorb(slot, c):
        m_run = mrun_scr[0:1, :]
        m_new = jnp.maximum(m_run, jnp.max(top_scr[slot], axis=0, keepdims=True))
        pr = jnp.exp2(lg_scr[slot] - m_new)
        ct = cbt_ref[pl.ds(c * KEY_TILES, KEY_TILES)]
        ct = jnp.concatenate([ct[t] for t in range(KEY_TILES)], axis=1)
        oacc_scr[...] = oacc_scr[...] * jnp.exp2(m_run - m_new) + _dot(ct, pr.astype(BF16))
        mrun_scr[...] = jnp.broadcast_to(m_new, (SUBLANES, wide))

    lgm = _dot(cm_ref[...], qt) + tm_ref[jnp.where(i == 0, 1, 0)]
    m_meta = jnp.max(lgm, axis=0, keepdims=True)
    mrun_scr[...] = jnp.broadcast_to(m_meta, (SUBLANES, wide))
    oacc_scr[...] = _dot(cmt_ref[...], jnp.exp2(lgm - m_meta).astype(BF16))

    make_logits(0, 0)

    def pair_body(p, carry):
        make_logits(1, 2 * p + 1)
        absorb(0, 2 * p)
        make_logits(0, 2 * p + 2)
        absorb(1, 2 * p + 1)
        return carry

    pairs = (nch - 1) // 2
    lax.fori_loop(0, pairs, pair_body, 0)
    rest_two = nch - 2 * pairs == 2

    @pl.when(rest_two)
    def _():
        make_logits(1, nch - 1)

    absorb(0, 2 * pairs)

    @pl.when(rest_two)
    def _():
        absorb(1, nch - 1)

    o = (oacc_scr[:KV_RANK, :] / oacc_scr[KV_RANK:KV_RANK + 1, :]).astype(BF16)
    o_all = jnp.concatenate([o[:, h * QB:(h + 1) * QB] for h in range(ATTN_HEADS)], axis=0)
    a_t = _dot(wuvt_ref[...], o_all)
    a_ref[...] = a_t.T * ga_ref[...]


def _attention(k_top, batch, seq, qit, kib, wt, qlt, cb, cbt, cm, cmt, ga, tb, tm, wuvt):
    nq = seq // QB
    full = lambda a: pl.BlockSpec(a.shape, lambda b, i: (0,) * a.ndim)
    qcols = lambda n: pl.BlockSpec((n, QB), lambda b, i: (0, b * nq + i))
    per_batch = lambda w: pl.BlockSpec((seq, w), lambda b, i: (b, 0))
    return pl.pallas_call(
        functools.partial(_attn_kernel, k_top),
        grid=(batch, nq),
        in_specs=[
            qcols(IDX_HEADS * IDX_DIM),
            per_batch(IDX_DIM),
            qcols(SUBLANES),
            pl.BlockSpec((ATTN_HEADS, KV_RANK, QB), lambda b, i: (0, 0, b * nq + i)),
            per_batch(KV_RANK),
            pl.BlockSpec((nq, PV_ROWS, QB), lambda b, i: (b, 0, 0)),
            full(cm), full(cmt),
            pl.BlockSpec((QB, ATTN_WIDTH), lambda b, i: (b * nq + i, 0)),
            full(tb), full(tm), full(wuvt),
        ],
        out_specs=pl.BlockSpec((QB, ATTN_WIDTH), lambda b, i: (b * nq + i, 0)),
        out_shape=jax.ShapeDtypeStruct((batch * seq, ATTN_WIDTH), F32),
        scratch_shapes=[
            pltpu.VMEM((seq, QB), F32),
            pltpu.VMEM((seq, QB), jnp.int32),
            pltpu.VMEM((seq // WORD_BITS, QB), jnp.int32),
            pltpu.VMEM((seq // WORD_BITS, QB), jnp.int32),
            pltpu.VMEM((SUBLANES, QB), F32),
            pltpu.VMEM((SUBLANES, QB), F32),
            pltpu.VMEM((seq, QB), F32),
            pltpu.VMEM((2, CK, ATTN_HEADS * QB), F32),
            pltpu.VMEM((2, SUBLANES, ATTN_HEADS * QB), F32),
            pltpu.VMEM((SUBLANES, ATTN_HEADS * QB), F32),
            pltpu.VMEM((PV_ROWS, ATTN_HEADS * QB), F32),
        ],
        compiler_params=_params("arbitrary", "arbitrary"),
        name="dsa_attention",
    )(qit, kib, wt, qlt, cb, cbt, cm, cmt, ga, tb, tm, wuvt)


def _meta_attn_kernel(qlt_ref, cm_ref, cmt_ref, ga_ref, tb_ref, wuvt_ref, a_ref, o_scr):
    cm = cm_ref[...]
    cmt = cmt_ref[...]
    k = lax.broadcasted_iota(jnp.int32, (N_META, N_META), 0)
    q = lax.broadcasted_iota(jnp.int32, (N_META, N_META), 1)
    for h in range(ATTN_HEADS):
        lg = _dot(cm, qlt_ref[h]) + tb_ref[2, :N_META, h * QB:h * QB + N_META]
        lg = jnp.where(k <= q, lg, -jnp.inf)
        pr = jnp.exp2(lg - jnp.max(lg, axis=0, keepdims=True))
        pr = pr / jnp.sum(pr, axis=0, keepdims=True)
        o_scr[h * KV_RANK:(h + 1) * KV_RANK, :] = _dot(cmt, pr.astype(BF16)).astype(BF16)
    a_ref[...] = _dot(wuvt_ref[...], o_scr[...]).T * ga_ref[...]


def _meta_attention(qlt_m, cm, cmt, ga_m, tb, wuvt):
    return pl.pallas_call(
        _meta_attn_kernel,
        out_shape=jax.ShapeDtypeStruct((N_META, ATTN_WIDTH), F32),
        scratch_shapes=[pltpu.VMEM((ATTN_HEADS * KV_RANK, N_META), BF16)],
        name="meta_attention",
    )(qlt_m, cm, cmt, ga_m, tb, wuvt)


def _hgrn_levels(chunk):
    return [1 << p for p in range(int(math.log2(chunk)))]


def _hgrn_masks(chunk):
    t = np.arange(chunk)[:, None]
    s = np.arange(chunk)[None, :]
    masks = [t == s]
    for m in _hgrn_levels(chunk):
        same = (t // (2 * m)) == (s // (2 * m))
        masks.append(same & ((t & m) != 0) & ((s & m) == 0))
    return np.stack(masks).astype(np.float32)


def _block_ref_rows(b, m, chunk, row):
    if 2 * m >= SUBLANES:
        pieces = []
        for blk in range(chunk // (2 * m)):
            r = blk * 2 * m + m - 1
            pieces.append(jnp.broadcast_to(b[r:r + 1, :], (2 * m, b.shape[1])))
        return pieces[0] if len(pieces) == 1 else jnp.concatenate(pieces, axis=0)
    delta = (row & (2 * m - 1)) - (m - 1)
    out = b
    for d in range(-(m - 1), m + 1):
        if d != 0:
            out = jnp.where(delta == d, pltpu.roll(b, d % chunk, 0), out)
    return out


def _split3(x):
    hi = x.astype(BF16)
    r1 = x - hi.astype(F32)
    mid = r1.astype(BF16)
    lo = (r1 - mid.astype(F32)).astype(BF16)
    return hi, mid, lo


def _hgrn_kernel(chunk, nsub, qf_ref, g_ref, kk_ref, v_ref, gh_ref, s0_ref, ng_ref, tri_ref, lm_ref,
                 r_ref, sfin_ref, st_scr):
    ci = pl.program_id(1)

    @pl.when(ci == 0)
    def _():
        st_scr[...] = s0_ref[...]

    tri = tri_ref[...]
    row = lax.broadcasted_iota(jnp.int32, (chunk, HGRN_EXPAND), 0)
    levels = _hgrn_levels(chunk)
    side = [jnp.where((row & m) != 0, 1.0, -1.0) for m in levels]
    for hd, sub in [(hd, sub) for hd in range(HGRN_HEADS) for sub in range(nsub)]:
        sl = (slice(sub * chunk, (sub + 1) * chunk), slice(hd * HGRN_EXPAND, (hd + 1) * HGRN_EXPAND))
        q = qf_ref[sl]
        g = g_ref[sl]
        k = kk_ref[sl]
        vb = v_ref[sl].astype(BF16)
        g_hi, g_mid, g_lo = _split3(g)
        b = _dot(tri, g_hi) + _dot(tri, g_mid) + _dot(tri, g_lo)

        a = _dot_nt(q.astype(BF16), k.astype(BF16)) * lm_ref[0]
        for li, m in enumerate(levels):
            e = jnp.exp2((b - _block_ref_rows(b, m, chunk, row)) * side[li])
            a = a + _dot_nt((q * e).astype(BF16), (k * e).astype(BF16)) * lm_ref[li + 1]

        st = st_scr[hd]
        o = _dot_nt((q * jnp.exp2(b)).astype(BF16), st.astype(BF16)) + _dot(a.astype(BF16), vb)
        b_last = b[chunk - 1:chunk, :]
        kd = (k * jnp.exp2(b_last - b)).astype(BF16)
        st_scr[hd] = st * jnp.exp2(b_last) + _dot_tn(vb, kd)

        rn = o * lax.rsqrt(jnp.mean(o * o, axis=-1, keepdims=True) + EPS) * ng_ref[...]
        r_ref[sl] = rn * gh_ref[sl]

    @pl.when(ci == pl.num_programs(1) - 1)
    def _():
        sfin_ref[0] = st_scr[...]


def _hgrn(batch, seq, chunk, nsub, qf, g, kk, v, gh, s0, ng):
    nc = seq // (chunk * nsub)
    tri = jnp.asarray(np.tril(np.ones((chunk, chunk), np.float32)), BF16)
    lm = jnp.asarray(_hgrn_masks(chunk))
    blk = pl.BlockSpec((chunk * nsub, HGRN_WIDTH), lambda b, c: (b * nc + c, 0))
    full = lambda a: pl.BlockSpec(a.shape, lambda b, c: (0,) * a.ndim)
    return pl.pallas_call(
        functools.partial(_hgrn_kernel, chunk, nsub),
        grid=(batch, nc),
        in_specs=[blk, blk, blk, blk, blk, full(s0), full(ng), full(tri), full(lm)],
        out_specs=(blk, pl.BlockSpec((1, HGRN_HEADS, HGRN_EXPAND, HGRN_EXPAND),
                                     lambda b, c: (b, 0, 0, 0))),
        out_shape=(jax.ShapeDtypeStruct((batch * seq, HGRN_WIDTH), F32),
                   jax.ShapeDtypeStruct((batch, HGRN_HEADS, HGRN_EXPAND, HGRN_EXPAND), F32)),
        scratch_shapes=[pltpu.VMEM((HGRN_HEADS, HGRN_EXPAND, HGRN_EXPAND), F32)],
        compiler_params=_params("arbitrary", "arbitrary"),
        name=f"hgrn2_c{chunk}x{nsub}",
    )(qf, g, kk, v, gh, s0, ng, tri, lm)


def _out_kernel(a_ref, r_ref, h_ref, wo_ref, lg_ref, lb_ref, o_ref):
    y = (_dot(a_ref[...].astype(BF16), wo_ref[:ATTN_WIDTH, :])
         + _dot(r_ref[...].astype(BF16), wo_ref[ATTN_WIDTH:, :]))
    z = DN_ALPHA * h_ref[...] + y
    mu = jnp.mean(z, axis=-1, keepdims=True)
    zc = z - mu
    var = jnp.mean(zc * zc, axis=-1, keepdims=True)
    o_ref[...] = zc * lax.rsqrt(var + EPS) * lg_ref[...] + lb_ref[...]


def _out_project(a, r, h2d, wo, lng, lnb, rows):
    t = h2d.shape[0]
    row_spec = lambda w: pl.BlockSpec((rows, w), lambda i: (i, 0))
    full = lambda x: pl.BlockSpec(x.shape, lambda i: (0,) * x.ndim)
    return pl.pallas_call(
        _out_kernel,
        grid=(t // rows,),
        in_specs=[row_spec(ATTN_WIDTH), row_spec(HGRN_WIDTH), row_spec(D_MODEL),
                  full(wo), full(lng), full(lnb)],
        out_specs=row_spec(D_MODEL),
        out_shape=jax.ShapeDtypeStruct((t, D_MODEL), F32),
        compiler_params=_params("arbitrary"),
        name=f"out_proj_r{rows}",
    )(a, r, h2d, wo, lng, lnb)


def _block_diag(blocks):
    n, r, c = blocks.shape
    eye = jnp.eye(n, dtype=blocks.dtype)
    return (eye[:, None, :, None] * blocks[:, :, None, :]).reshape(n * r, n * c)


def kernel(x, meta_tokens, rel_bias, hgrn_lb_raw, w_in, kv_norm_g, w_uk, w_uv, hgrn_norm_g, w_out, ln_g, ln_b):
    batch, seq, _ = x.shape
    assert seq % CK == 0 and seq % (HGRN_CHUNK * HGRN_STEP_CHUNKS) == 0 and (batch * seq) % OUT_ROWS == 0
    assert seq // GROUP < WORD_BITS
    k_top = min(TOPK_MAX, seq // 4)

    rel_bias = rel_bias.astype(F32)
    tb, tm = _bias_tables(rel_bias)
    lbraw = hgrn_lb_raw.astype(F32)
    h = x.reshape(batch * seq, D_MODEL).astype(F32)
    hm = meta_tokens.astype(F32)
    s_zero = jnp.zeros((HGRN_HEADS, HGRN_EXPAND, HGRN_EXPAND), F32)

    wp_all = jnp.concatenate(
        [w_in[:, :, :_RAW_SPLIT], jnp.zeros((DEPTH, D_MODEL, _P_GA - _RAW_SPLIT), w_in.dtype),
         w_in[:, :, _RAW_SPLIT:]], axis=2).astype(BF16)

    for l in range(DEPTH):
        wp = wp_all[l]
        wuk_bd = _block_diag(w_uk[l]).astype(BF16)
        wuvt = _block_diag(w_uv[l]).T.astype(BF16)
        wo = w_out[l].astype(BF16)
        kvg = kv_norm_g[l].reshape(1, KV_RANK).astype(F32)
        ng = hgrn_norm_g[l].reshape(1, HGRN_EXPAND).astype(F32)
        lng = ln_g[l].reshape(1, D_MODEL).astype(F32)
        lnb = ln_b[l].reshape(1, D_MODEL).astype(F32)

        hm_pad = jnp.pad(hm, ((0, QB - N_META), (0, 0)))
        (qlt_m, cb_m, cbt_m, _, _, _, ga_m, qf_m, g_m, kk_m, v_m, gh_m) = _project(
            l, hm_pad, wp, wuk_bd, kvg, lbraw, QB)
        cm = cb_m[:N_META]
        cmt = cbt_m[0][:, :N_META]
        a_m = _meta_attention(qlt_m[:, :, :N_META], cm, cmt[:KV_RANK], ga_m[:N_META], tb, wuvt)
        r_m, s_m = _hgrn(1, N_META, N_META, 1, qf_m[:N_META], g_m[:N_META], kk_m[:N_META],
                         v_m[:N_META], gh_m[:N_META], s_zero, ng)

        (qlt, cb, cbt, qit, kib, wt, ga, qf, g, kk, v, gh) = _project(
            l, h, wp, wuk_bd, kvg, lbraw, PROJ_ROWS)
        a = _attention(k_top, batch, seq, qit, kib, wt, qlt, cb, cbt, cm, cmt, ga, tb, tm, wuvt)
        r, _ = _hgrn(batch, seq, HGRN_CHUNK, HGRN_STEP_CHUNKS, qf, g, kk, v, gh, s_m[0], ng)

        hm = _out_project(a_m, r_m, hm, wo, lng, lnb, N_META)
        h = _out_project(a, r, h, wo, lng, lnb, OUT_ROWS)

    return h.reshape(batch, seq, D_MODEL).astype(x.dtype)
```

```python
import functools
import math

import numpy as np
import jax
import jax.numpy as jnp
from jax import lax
from jax.experimental import pallas as pl
from jax.experimental.pallas import tpu as pltpu

D_MODEL = 1024
DEPTH = 2
N_META = 16
ATTN_WIDTH = 512
HGRN_WIDTH = 512
ATTN_HEADS = 8
ATTN_HEAD_DIM = 64
KV_RANK = 128
IDX_HEADS = 4
IDX_DIM = 64
TOPK_MAX = 256
HGRN_EXPAND = 128
HGRN_HEADS = 4
REL_BUCKETS = 32
REL_MAX_DIST = 128
DN_ALPHA = (2 * DEPTH) ** 0.25
EPS = 1e-6
LOG2E = math.log2(math.e)

F32 = jnp.float32
BF16 = jnp.bfloat16

LANES = 128
SUBLANES = 8
QB = 128
KEY_TILES = 4
CK = KEY_TILES * QB
WORD_BITS = 32
GROUP = WORD_BITS * SUBLANES
PV_ROWS = KV_RANK + 16
HGRN_CHUNK = 128
HGRN_STEP_CHUNKS = 4
PROJ_ROWS = 512
OUT_ROWS = 1024
VMEM_LIMIT = 40 * 1024 * 1024

_P_QA, _P_CKV, _P_QI, _P_KW, _P_GA, _P_QH, _P_FH, _P_IH, _P_GH, _P_END = (
    0, 512, 640, 896, 1024, 1536, 2048, 2560, 3072, 3584)
_RAW_SPLIT = 964

NEG_INF_KEY = -2139095041
_WORD_INDEX_LOW = (0xFFFF0000, 0xFF00FF00, 0xF0F0F0F0, 0xCCCCCCCC, 0xAAAAAAAA)
_WORD_BIT = [int(np.uint32(1 << (31 - j)).astype(np.int32)) for j in range(32)]
INT_MIN = -2147483648


def _bucket_starts():
    max_exact = REL_BUCKETS // 2
    n = np.arange(0, 4 * REL_MAX_DIST)
    nf = np.maximum(n, 1).astype(np.float64)
    large = max_exact + (np.log(nf / max_exact) / math.log(REL_MAX_DIST / max_exact)
                         * (REL_BUCKETS - max_exact)).astype(np.int64)
    large = np.minimum(large, REL_BUCKETS - 1)
    bucket = np.where(n < max_exact, n, large)
    return [int(np.argmax(bucket >= b)) for b in range(REL_BUCKETS)]


_BUCKET_START = _bucket_starts()


def _dot(a, b):
    return jnp.dot(a, b, preferred_element_type=F32)


def _dot_nt(a, b):
    return lax.dot_general(a, b, (((1,), (1,)), ((), ())), preferred_element_type=F32)


def _dot_tn(a, b):
    return lax.dot_general(a, b, (((0,), (0,)), ((), ())), preferred_element_type=F32)


def _sigmoid(x):
    return 1.0 / (1.0 + jnp.exp(-x))


def _params(*sem):
    return pltpu.CompilerParams(dimension_semantics=sem, vmem_limit_bytes=VMEM_LIMIT)


def _fold_rows(x, op):
    parts = [x[r:r + SUBLANES] for r in range(0, x.shape[0], SUBLANES)]
    while len(parts) > 1:
        pairs = [op(parts[k], parts[k + 1]) for k in range(0, len(parts) - 1, 2)]
        parts = pairs + parts[len(parts) - len(parts) % 2:]
    return parts[0]


def _proj_kernel(layer, h_ref, wp_ref, wuk_ref, kvg_ref, lbraw_ref,
                 qlt_ref, cb_ref, cbt_ref, qit_ref, kib_ref, wt_ref, ga_ref,
                 qf_ref, g_ref, kk_ref, v_ref, gh_ref):
    rows = h_ref.shape[0]
    hb = h_ref[...].astype(BF16)

    def proj(lo, hi):
        return _dot(hb, wp_ref[:, lo:hi])

    qa = proj(_P_QA, _P_CKV).astype(BF16)
    pair_in = 2 * ATTN_HEAD_DIM
    ql = jnp.concatenate(
        [_dot(qa[:, p * pair_in:(p + 1) * pair_in], wuk_ref[p]) for p in range(ATTN_HEADS // 2)],
        axis=1) * (ATTN_HEAD_DIM ** -0.5 * LOG2E)
    qlt_ref[...] = ql.T.reshape(ATTN_HEADS, KV_RANK, rows).astype(BF16)

    ckv = proj(_P_CKV, _P_QI)
    c = ckv * lax.rsqrt(jnp.mean(ckv * ckv, axis=-1, keepdims=True) + EPS) * kvg_ref[...]
    cb_ref[...] = c.astype(BF16)
    ct = c.T
    ones_row = jnp.where(lax.broadcasted_iota(jnp.int32, (PV_ROWS - KV_RANK, QB), 0) == 0, 1.0, 0.0)
    for t in range(rows // QB):
        cbt_ref[t] = jnp.concatenate([ct[:, t * QB:(t + 1) * QB], ones_row], axis=0).astype(BF16)

    qit_ref[...] = proj(_P_QI, _P_KW).T.astype(BF16)
    kw = proj(_P_KW, _P_GA)
    kib_ref[...] = kw[:, :IDX_DIM].astype(BF16)
    wt_ref[...] = kw.T[IDX_DIM:IDX_DIM + SUBLANES, :] * (IDX_HEADS ** -0.5 * IDX_DIM ** -0.5)

    ga = proj(_P_GA, _P_QH)
    ga_ref[...] = ga * _sigmoid(ga)

    qh = proj(_P_QH, _P_FH)
    qf_ref[...] = qh * _sigmoid(qh) * (HGRN_EXPAND ** -0.5)

    raw = lbraw_ref[...]
    ex = jnp.exp(raw - jnp.max(raw, axis=0, keepdims=True))
    lbp = ex / jnp.sum(ex, axis=0, keepdims=True)
    lb = lbp[0:1]
    for i in range(1, layer + 1):
        lb = lb + lbp[i:i + 1]
    lb = lb - lbp[0:1]
    f = lb + (1.0 - lb) * _sigmoid(proj(_P_FH, _P_IH))
    g_ref[...] = jnp.log2(f)
    kk_ref[...] = 1.0 - f
    v_ref[...] = proj(_P_IH, _P_GH)
    gh = proj(_P_GH, _P_END)
    gh_ref[...] = gh * _sigmoid(gh)


def _project(layer, h2d, wp, wuk_bd, kvg, lbraw, rows):
    t = h2d.shape[0]
    grid = (t // rows,)
    row_spec = lambda w: pl.BlockSpec((rows, w), lambda r: (r, 0))
    col_spec = lambda n: pl.BlockSpec((n, rows), lambda r: (0, r))
    full = lambda a: pl.BlockSpec(a.shape, lambda r: (0,) * a.ndim)
    out_shape = (
        jax.ShapeDtypeStruct((ATTN_HEADS, KV_RANK, t), BF16),
        jax.ShapeDtypeStruct((t, KV_RANK), BF16),
        jax.ShapeDtypeStruct((t // QB, PV_ROWS, QB), BF16),
        jax.ShapeDtypeStruct((IDX_HEADS * IDX_DIM, t), BF16),
        jax.ShapeDtypeStruct((t, IDX_DIM), BF16),
        jax.ShapeDtypeStruct((SUBLANES, t), F32),
        jax.ShapeDtypeStruct((t, ATTN_WIDTH), F32),
        jax.ShapeDtypeStruct((t, HGRN_WIDTH), F32),
        jax.ShapeDtypeStruct((t, HGRN_WIDTH), F32),
        jax.ShapeDtypeStruct((t, HGRN_WIDTH), F32),
        jax.ShapeDtypeStruct((t, HGRN_WIDTH), F32),
        jax.ShapeDtypeStruct((t, HGRN_WIDTH), F32),
    )
    out_specs = (
        pl.BlockSpec((ATTN_HEADS, KV_RANK, rows), lambda r: (0, 0, r)),
        row_spec(KV_RANK),
        pl.BlockSpec((rows // QB, PV_ROWS, QB), lambda r: (r, 0, 0)),
        col_spec(IDX_HEADS * IDX_DIM), row_spec(IDX_DIM), col_spec(SUBLANES),
        row_spec(ATTN_WIDTH), row_spec(HGRN_WIDTH), row_spec(HGRN_WIDTH), row_spec(HGRN_WIDTH),
        row_spec(HGRN_WIDTH), row_spec(HGRN_WIDTH),
    )
    return pl.pallas_call(
        functools.partial(_proj_kernel, layer),
        grid=grid,
        in_specs=[row_spec(D_MODEL), full(wp), full(wuk_bd), full(kvg), full(lbraw)],
        out_specs=out_specs,
        out_shape=out_shape,
        compiler_params=_params("arbitrary"),
        name=f"proj_l{layer}_r{rows}",
    )(h2d, wp, wuk_bd, kvg, lbraw)


def _bias_tile(dist, rb_ref, h):
    out = jnp.full(dist.shape, rb_ref[0, h], F32)
    for b in range(1, REL_BUCKETS):
        out = jnp.where(dist >= _BUCKET_START[b], rb_ref[b, h], out)
    return out


def _bias_kernel(rb_ref, tb_ref, tm_ref):
    k = lax.broadcasted_iota(jnp.int32, (QB, QB), 0)
    q = lax.broadcasted_iota(jnp.int32, (QB, QB), 1)
    km = lax.broadcasted_iota(jnp.int32, (N_META, QB), 0)
    qm = lax.broadcasted_iota(jnp.int32, (N_META, QB), 1)
    for h in range(ATTN_HEADS):
        lanes = slice(h * QB, (h + 1) * QB)
        far = rb_ref[REL_BUCKETS - 1, h] * LOG2E
        tb_ref[0, :, lanes] = jnp.full((QB, QB), far, F32)
        tb_ref[1, :, lanes] = _bias_tile(q - k + QB, rb_ref, h) * LOG2E
        tb_ref[2, :, lanes] = _bias_tile(jnp.maximum(q - k, 0), rb_ref, h) * LOG2E
        tm_ref[0, :, lanes] = jnp.full((N_META, QB), far, F32)
        tm_ref[1, :, lanes] = _bias_tile(N_META + qm - km, rb_ref, h) * LOG2E


def _bias_tables(rel_bias):
    wide = ATTN_HEADS * QB
    return pl.pallas_call(
        _bias_kernel,
        in_specs=[pl.BlockSpec(memory_space=pltpu.SMEM)],
        out_shape=(jax.ShapeDtypeStruct((3, QB, wide), F32), jax.ShapeDtypeStruct((2, N_META, wide), F32)),
        name="bias_tables",
    )(rel_bias)


def _pair_project(wuvt_ref, o_all):
    rows = 2 * KV_RANK
    return jnp.concatenate(
        [_dot(wuvt_ref[p], o_all[p * rows:(p + 1) * rows]) for p in range(ATTN_HEADS // 2)], axis=0)


def _to_ukey(s):
    bits = pltpu.bitcast(s, jnp.int32)
    return bits ^ ((bits >> 31) | INT_MIN)


def _bit_transpose(x):
    x = list(x)
    j, m = 16, 0x0000FFFF
    while j:
        k = 0
        while k < WORD_BITS:
            t = (x[k] ^ lax.shift_right_logical(x[k + j], jnp.int32(j))) & m
            x[k] = x[k] ^ t
            x[k + j] = x[k + j] ^ jnp.left_shift(t, j)
            k = (k + j + 1) & ~j
        j >>= 1
        m = (m ^ (m << j)) & 0xFFFFFFFF
    return x


def _attn_kernel(k_top, qit_ref, kib_ref, wt_ref, qlt_ref, cb_ref, cbt_ref, cm_ref, cmt_ref, ga_ref,
                 tb_ref, tm_ref, wuvt_ref, a_ref,
                 s_scr, plane_scr, tie_scr, sel_scr, thr_scr, ngt_scr, am_scr, lg_scr, top_scr, mrun_scr, oacc_scr):
    i = pl.program_id(1)
    nch = i // KEY_TILES + 1
    wide = ATTN_HEADS * QB
    prows = plane_scr.shape[0] // WORD_BITS

    def rows_of(c):
        return pl.ds(pl.multiple_of(c * CK, CK), CK)

    def tile_rows(j):
        return pl.ds(pl.multiple_of(j * QB, QB), QB)

    def group_word(ref, g):
        return ref[pl.ds(pl.multiple_of(g * SUBLANES, SUBLANES), SUBLANES), :]

    qit = qit_ref[...]
    wt = wt_ref[...]
    k_loc = lax.broadcasted_iota(jnp.int32, (CK, QB), 0)
    q_pos = i * QB + lax.broadcasted_iota(jnp.int32, (CK, QB), 1)

    def score_chunk(c, last):
        kc = kib_ref[rows_of(c), :]
        s = jnp.maximum(_dot(kc, qit[0:IDX_DIM]), 0.0) * wt[0:1]
        for h in range(1, IDX_HEADS):
            s = s + jnp.maximum(_dot(kc, qit[h * IDX_DIM:(h + 1) * IDX_DIM]), 0.0) * wt[h:h + 1]
        if last:
            s = jnp.where(c * CK + k_loc <= q_pos, s, -jnp.inf)
        s_scr[rows_of(c), :] = s
        u = _to_ukey(s)
        for gl in range(CK // GROUP):
            words = [u[gl * GROUP + j * SUBLANES:gl * GROUP + (j + 1) * SUBLANES] for j in range(WORD_BITS)]
            g = c * (CK // GROUP) + gl
            for p, word in enumerate(_bit_transpose(words)):
                plane_scr[pl.ds(pl.multiple_of(p * prows + g * SUBLANES, SUBLANES), SUBLANES), :] = word

    def score_body(p, carry):
        score_chunk(2 * p, False)
        score_chunk(2 * p + 1, False)
        return carry

    lax.fori_loop(0, (nch - 1) // 2, score_body, 0)

    @pl.when((nch - 1) % 2 == 1)
    def _():
        score_chunk(nch - 2, False)

    score_chunk(nch - 1, True)

    kf = float(k_top)
    group = lax.broadcasted_iota(jnp.int32, (prows, QB), 0) // SUBLANES
    active = jnp.where(group < nch * (CK // GROUP), -1, 0)

    def ones_per_query(x):
        return jnp.sum(_fold_rows(lax.population_count(x), jnp.add).astype(F32), axis=0, keepdims=True)

    def from_ukey(u):
        return pltpu.bitcast(jnp.where(u < 0, u ^ INT_MIN, ~u), F32)

    def bit_body(p, carry):
        alive, want, thr = carry
        hi = alive & plane_scr[pl.ds(pl.multiple_of(p * prows, prows), prows), :]
        n_hi = ones_per_query(hi)
        up = n_hi >= want
        return (jnp.where(up, hi, alive ^ hi), jnp.where(up, want, want - n_hi),
                thr | jnp.where(up, jnp.left_shift(jnp.int32(1), WORD_BITS - 1 - p), 0))

    _, _, thr_bits = lax.fori_loop(
        0, WORD_BITS, bit_body,
        (active, jnp.full((1, QB), kf, F32), jnp.zeros((1, QB), jnp.int32)))

    def census(thr_f):
        def body(c, acc):
            s = s_scr[rows_of(c), :]
            for gl in range(CK // GROUP):
                bits = [jnp.where(s[gl * GROUP + j * SUBLANES:gl * GROUP + (j + 1) * SUBLANES] == thr_f,
                                  _WORD_BIT[j], 0) for j in range(WORD_BITS)]
                tie_scr[pl.ds(pl.multiple_of((c * (CK // GROUP) + gl) * SUBLANES, SUBLANES), SUBLANES), :] = (
                    _fold_rows(jnp.concatenate(bits, axis=0), jnp.bitwise_or))
            return acc + jnp.sum(jnp.where(s > thr_f, 1.0, 0.0).reshape(KEY_TILES, QB, QB), axis=0)
        acc = lax.fori_loop(0, nch, body, jnp.zeros((QB, QB), F32))
        return jnp.sum(acc, axis=0, keepdims=True)

    def keep(thr_f, n_gt):
        thr_scr[...] = jnp.broadcast_to(thr_f, (SUBLANES, QB))
        ngt_scr[...] = jnp.broadcast_to(n_gt, (SUBLANES, QB))

    thr_fast = from_ukey(thr_bits)
    n_gt_fast = census(thr_fast)
    n_eq_fast = ones_per_query(tie_scr[...] & active)
    keep(thr_fast, n_gt_fast)
    unconfirmed = (n_gt_fast >= kf) | (n_gt_fast + n_eq_fast < kf)

    @pl.when(jnp.max(jnp.where(unconfirmed, 1.0, 0.0)) > 0.5)
    def _():
        def count_ge(cand_f):
            def body(c, acc):
                hit = jnp.where(s_scr[rows_of(c), :] >= cand_f, 1.0, 0.0)
                return acc + jnp.sum(hit.reshape(KEY_TILES, QB, QB), axis=0)
            acc = lax.fori_loop(0, nch, body, jnp.zeros((QB, QB), F32))
            return jnp.sum(acc, axis=0, keepdims=True)

        def slow_body(p, prefix):
            cand = prefix | jnp.left_shift(jnp.int32(1), WORD_BITS - 1 - p)
            floor = jnp.where(cand >= 0, jnp.maximum(cand, NEG_INF_KEY ^ INT_MIN), cand)
            return jnp.where(count_ge(from_ukey(floor)) >= kf, cand, prefix)

        thr_slow = from_ukey(lax.fori_loop(0, WORD_BITS, slow_body, jnp.zeros((1, QB), jnp.int32)))
        keep(thr_slow, census(thr_slow))

    thr_f = thr_scr[0:1, :]
    n_gt = ngt_scr[0:1, :]

    rest = jnp.where(thr_f == -jnp.inf, 0.0, kf - n_gt)
    alive = tie_scr[...] & active
    row = lax.broadcasted_iota(jnp.int32, (prows, QB), 0)
    group_bits = [1 << b for b in reversed(range((prows // SUBLANES - 1).bit_length()))]
    low_half = ([jnp.where((row & (SUBLANES * gb)) == 0, -1, 0) for gb in group_bits]
                + [int(np.uint32(m).astype(np.int32)) for m in _WORD_INDEX_LOW]
                + [jnp.where((row & rb) == 0, -1, 0) for rb in (4, 2, 1)])
    ties, taken = alive, jnp.zeros((prows, QB), jnp.int32)
    for low in low_half:
        lo = ties & low
        n_lo = ones_per_query(lo)
        inside = n_lo >= rest
        taken = taken | jnp.where(inside, 0, lo)
        ties = jnp.where(inside, lo, ties ^ lo)
        rest = jnp.where(inside, rest, rest - n_lo)
    sel_scr[...] = taken | jnp.where(rest >= 1.0, ties, 0)

    def mask_body(c, carry):
        s = s_scr[rows_of(c), :]
        for gl in range(CK // GROUP):
            g = c * (CK // GROUP) + gl
            word = group_word(sel_scr, g)
            am_scr[pl.ds(pl.multiple_of(g * GROUP, GROUP), GROUP), :] = jnp.concatenate(
                [jnp.where(s[gl * GROUP + j * SUBLANES:gl * GROUP + (j + 1) * SUBLANES] > thr_f, 0.0,
                           jnp.where(jnp.left_shift(word, j) < 0, 0.0, -jnp.inf))
                 for j in range(WORD_BITS)], axis=0)
        return carry

    lax.fori_loop(0, nch, mask_body, 0)

    qt = jnp.concatenate([qlt_ref[h] for h in range(ATTN_HEADS)], axis=1)

    def make_logits(slot, c):
        lg = _dot(cb_ref[rows_of(c), :], qt)
        top = jnp.full((SUBLANES, wide), -jnp.inf, F32)
        for t in range(KEY_TILES):
            j = c * KEY_TILES + t
            kind = jnp.where(j == i, 2, jnp.where(j == i - 1, 1, 0))
            am = am_scr[tile_rows(j), :]
            lgt = lg[t * QB:(t + 1) * QB] + tb_ref[kind] + jnp.concatenate([am] * ATTN_HEADS, axis=1)
            lg_scr[slot, t * QB:(t + 1) * QB, :] = lgt
            top = jnp.maximum(top, _fold_rows(lgt, jnp.maximum))
        top_scr[slot] = top

    def absorb(slot, c):
        m_run = mrun_scr[0:1, :]
        m_new = jnp.maximum(m_run, jnp.max(top_scr[slot], axis=0, keepdims=True))
        pr = jnp.exp2(lg_scr[slot] - m_new)
        ct = cbt_ref[pl.ds(c * KEY_TILES, KEY_TILES)]
        ct = jnp.concatenate([ct[t] for t in range(KEY_TILES)], axis=1)
        oacc_scr[...] = oacc_scr[...] * jnp.exp2(m_run - m_new) + _dot(ct, pr.astype(BF16))
        mrun_scr[...] = jnp.broadcast_to(m_new, (SUBLANES, wide))

    lgm = _dot(cm_ref[...], qt) + tm_ref[jnp.where(i == 0, 1, 0)]
    m_meta = jnp.max(lgm, axis=0, keepdims=True)
    mrun_scr[...] = jnp.broadcast_to(m_meta, (SUBLANES, wide))
    oacc_scr[...] = _dot(cmt_ref[...], jnp.exp2(lgm - m_meta).astype(BF16))

    make_logits(0, 0)

    def pair_body(p, carry):
        make_logits(1, 2 * p + 1)
        absorb(0, 2 * p)
        make_logits(0, 2 * p + 2)
        absorb(1, 2 * p + 1)
        return carry

    pairs = (nch - 1) // 2
    lax.fori_loop(0, pairs, pair_body, 0)
    rest_two = nch - 2 * pairs == 2

    @pl.when(rest_two)
    def _():
        make_logits(1, nch - 1)

    absorb(0, 2 * pairs)

    @pl.when(rest_two)
    def _():
        absorb(1, nch - 1)

    o = (oacc_scr[:KV_RANK, :] / oacc_scr[KV_RANK:KV_RANK + 1, :]).astype(BF16)
    o_all = jnp.concatenate([o[:, h * QB:(h + 1) * QB] for h in range(ATTN_HEADS)], axis=0)
    a_t = _pair_project(wuvt_ref, o_all)
    a_ref[...] = a_t.T * ga_ref[...]


def _attention(k_top, batch, seq, qit, kib, wt, qlt, cb, cbt, cm, cmt, ga, tb, tm, wuvt):
    nq = seq // QB
    full = lambda a: pl.BlockSpec(a.shape, lambda b, i: (0,) * a.ndim)
    qcols = lambda n: pl.BlockSpec((n, QB), lambda b, i: (0, b * nq + i))
    per_batch = lambda w: pl.BlockSpec((seq, w), lambda b, i: (b, 0))
    return pl.pallas_call(
        functools.partial(_attn_kernel, k_top),
        grid=(batch, nq),
        in_specs=[
            qcols(IDX_HEADS * IDX_DIM),
            per_batch(IDX_DIM),
            qcols(SUBLANES),
            pl.BlockSpec((ATTN_HEADS, KV_RANK, QB), lambda b, i: (0, 0, b * nq + i)),
            per_batch(KV_RANK),
            pl.BlockSpec((nq, PV_ROWS, QB), lambda b, i: (b, 0, 0)),
            full(cm), full(cmt),
            pl.BlockSpec((QB, ATTN_WIDTH), lambda b, i: (b * nq + i, 0)),
            full(tb), full(tm), full(wuvt),
        ],
        out_specs=pl.BlockSpec((QB, ATTN_WIDTH), lambda b, i: (b * nq + i, 0)),
        out_shape=jax.ShapeDtypeStruct((batch * seq, ATTN_WIDTH), F32),
        scratch_shapes=[
            pltpu.VMEM((seq, QB), F32),
            pltpu.VMEM((seq, QB), jnp.int32),
            pltpu.VMEM((seq // WORD_BITS, QB), jnp.int32),
            pltpu.VMEM((seq // WORD_BITS, QB), jnp.int32),
            pltpu.VMEM((SUBLANES, QB), F32),
            pltpu.VMEM((SUBLANES, QB), F32),
            pltpu.VMEM((seq, QB), F32),
            pltpu.VMEM((2, CK, ATTN_HEADS * QB), F32),
            pltpu.VMEM((2, SUBLANES, ATTN_HEADS * QB), F32),
            pltpu.VMEM((SUBLANES, ATTN_HEADS * QB), F32),
            pltpu.VMEM((PV_ROWS, ATTN_HEADS * QB), F32),
        ],
        compiler_params=_params("arbitrary", "arbitrary"),
        name="dsa_attention",
    )(qit, kib, wt, qlt, cb, cbt, cm, cmt, ga, tb, tm, wuvt)


def _meta_attn_kernel(qlt_ref, cm_ref, cmt_ref, ga_ref, tb_ref, wuvt_ref, a_ref, o_scr):
    cm = cm_ref[...]
    cmt = cmt_ref[...]
    k = lax.broadcasted_iota(jnp.int32, (N_META, N_META), 0)
    q = lax.broadcasted_iota(jnp.int32, (N_META, N_META), 1)
    for h in range(ATTN_HEADS):
        lg = _dot(cm, qlt_ref[h]) + tb_ref[2, :N_META, h * QB:h * QB + N_META]
        lg = jnp.where(k <= q, lg, -jnp.inf)
        pr = jnp.exp2(lg - jnp.max(lg, axis=0, keepdims=True))
        pr = pr / jnp.sum(pr, axis=0, keepdims=True)
        o_scr[h * KV_RANK:(h + 1) * KV_RANK, :] = _dot(cmt, pr.astype(BF16)).astype(BF16)
    a_ref[...] = _pair_project(wuvt_ref, o_scr[...]).T * ga_ref[...]


def _meta_attention(qlt_m, cm, cmt, ga_m, tb, wuvt):
    return pl.pallas_call(
        _meta_attn_kernel,
        out_shape=jax.ShapeDtypeStruct((N_META, ATTN_WIDTH), F32),
        scratch_shapes=[pltpu.VMEM((ATTN_HEADS * KV_RANK, N_META), BF16)],
        name="meta_attention",
    )(qlt_m, cm, cmt, ga_m, tb, wuvt)


def _hgrn_levels(chunk):
    return [1 << p for p in range(int(math.log2(chunk)))]


def _hgrn_masks(chunk):
    t = np.arange(chunk)[:, None]
    s = np.arange(chunk)[None, :]
    masks = [t == s]
    for m in _hgrn_levels(chunk):
        same = (t // (2 * m)) == (s // (2 * m))
        masks.append(same & ((t & m) != 0) & ((s & m) == 0))
    return np.stack(masks).astype(np.float32)


def _block_ref_rows(b, m, chunk, row):
    if 2 * m >= SUBLANES:
        pieces = []
        for blk in range(chunk // (2 * m)):
            r = blk * 2 * m + m - 1
            pieces.append(jnp.broadcast_to(b[r:r + 1, :], (2 * m, b.shape[1])))
        return pieces[0] if len(pieces) == 1 else jnp.concatenate(pieces, axis=0)
    delta = (row & (2 * m - 1)) - (m - 1)
    out = b
    for d in range(-(m - 1), m + 1):
        if d != 0:
            out = jnp.where(delta == d, pltpu.roll(b, d % chunk, 0), out)
    return out


def _split3(x):
    hi = x.astype(BF16)
    r1 = x - hi.astype(F32)
    mid = r1.astype(BF16)
    lo = (r1 - mid.astype(F32)).astype(BF16)
    return hi, mid, lo


def _hgrn_kernel(chunk, nsub, qf_ref, g_ref, kk_ref, v_ref, gh_ref, s0_ref, ng_ref, tri_ref, lm_ref,
                 r_ref, sfin_ref, st_scr):
    ci = pl.program_id(1)

    @pl.when(ci == 0)
    def _():
        st_scr[...] = s0_ref[...]

    tri = tri_ref[...]
    row = lax.broadcasted_iota(jnp.int32, (chunk, HGRN_EXPAND), 0)
    levels = _hgrn_levels(chunk)
    side = [jnp.where((row & m) != 0, 1.0, -1.0) for m in levels]
    for hd, sub in [(hd, sub) for hd in range(HGRN_HEADS) for sub in range(nsub)]:
        sl = (slice(sub * chunk, (sub + 1) * chunk), slice(hd * HGRN_EXPAND, (hd + 1) * HGRN_EXPAND))
        q = qf_ref[sl]
        g = g_ref[sl]
        k = kk_ref[sl]
        vb = v_ref[sl].astype(BF16)
        g_hi, g_mid, g_lo = _split3(g)
        b = _dot(tri, g_hi) + _dot(tri, g_mid) + _dot(tri, g_lo)

        a = _dot_nt(q.astype(BF16), k.astype(BF16)) * lm_ref[0]
        for li, m in enumerate(levels):
            e = jnp.exp2((b - _block_ref_rows(b, m, chunk, row)) * side[li])
            a = a + _dot_nt((q * e).astype(BF16), (k * e).astype(BF16)) * lm_ref[li + 1]

        st = st_scr[hd]
        o = _dot_nt((q * jnp.exp2(b)).astype(BF16), st.astype(BF16)) + _dot(a.astype(BF16), vb)
        b_last = b[chunk - 1:chunk, :]
        kd = (k * jnp.exp2(b_last - b)).astype(BF16)
        st_scr[hd] = st * jnp.exp2(b_last) + _dot_tn(vb, kd)

        rn = o * lax.rsqrt(jnp.mean(o * o, axis=-1, keepdims=True) + EPS) * ng_ref[...]
        r_ref[sl] = rn * gh_ref[sl]

    @pl.when(ci == pl.num_programs(1) - 1)
    def _():
        sfin_ref[0] = st_scr[...]


def _hgrn(batch, seq, chunk, nsub, qf, g, kk, v, gh, s0, ng):
    nc = seq // (chunk * nsub)
    tri = jnp.asarray(np.tril(np.ones((chunk, chunk), np.float32)), BF16)
    lm = jnp.asarray(_hgrn_masks(chunk))
    blk = pl.BlockSpec((chunk * nsub, HGRN_WIDTH), lambda b, c: (b * nc + c, 0))
    full = lambda a: pl.BlockSpec(a.shape, lambda b, c: (0,) * a.ndim)
    return pl.pallas_call(
        functools.partial(_hgrn_kernel, chunk, nsub),
        grid=(batch, nc),
        in_specs=[blk, blk, blk, blk, blk, full(s0), full(ng), full(tri), full(lm)],
        out_specs=(blk, pl.BlockSpec((1, HGRN_HEADS, HGRN_EXPAND, HGRN_EXPAND),
                                     lambda b, c: (b, 0, 0, 0))),
        out_shape=(jax.ShapeDtypeStruct((batch * seq, HGRN_WIDTH), F32),
                   jax.ShapeDtypeStruct((batch, HGRN_HEADS, HGRN_EXPAND, HGRN_EXPAND), F32)),
        scratch_shapes=[pltpu.VMEM((HGRN_HEADS, HGRN_EXPAND, HGRN_EXPAND), F32)],
        compiler_params=_params("arbitrary", "arbitrary"),
        name=f"hgrn2_c{chunk}x{nsub}",
    )(qf, g, kk, v, gh, s0, ng, tri, lm)


def _out_kernel(a_ref, r_ref, h_ref, wo_ref, lg_ref, lb_ref, o_ref):
    y = (_dot(a_ref[...].astype(BF16), wo_ref[:ATTN_WIDTH, :])
         + _dot(r_ref[...].astype(BF16), wo_ref[ATTN_WIDTH:, :]))
    z = DN_ALPHA * h_ref[...] + y
    mu = jnp.mean(z, axis=-1, keepdims=True)
    zc = z - mu
    var = jnp.mean(zc * zc, axis=-1, keepdims=True)
    o_ref[...] = zc * lax.rsqrt(var + EPS) * lg_ref[...] + lb_ref[...]


def _out_project(a, r, h2d, wo, lng, lnb, rows):
    t = h2d.shape[0]
    row_spec = lambda w: pl.BlockSpec((rows, w), lambda i: (i, 0))
    full = lambda x: pl.BlockSpec(x.shape, lambda i: (0,) * x.ndim)
    return pl.pallas_call(
        _out_kernel,
        grid=(t // rows,),
        in_specs=[row_spec(ATTN_WIDTH), row_spec(HGRN_WIDTH), row_spec(D_MODEL),
                  full(wo), full(lng), full(lnb)],
        out_specs=row_spec(D_MODEL),
        out_shape=jax.ShapeDtypeStruct((t, D_MODEL), F32),
        compiler_params=_params("arbitrary"),
        name=f"out_proj_r{rows}",
    )(a, r, h2d, wo, lng, lnb)


def _block_diag(blocks):
    n, r, c = blocks.shape
    eye = jnp.eye(n, dtype=blocks.dtype)
    return (eye[:, None, :, None] * blocks[:, :, None, :]).reshape(n * r, n * c)


def kernel(x, meta_tokens, rel_bias, hgrn_lb_raw, w_in, kv_norm_g, w_uk, w_uv, hgrn_norm_g, w_out, ln_g, ln_b):
    batch, seq, _ = x.shape
    assert seq % CK == 0 and seq % (HGRN_CHUNK * HGRN_STEP_CHUNKS) == 0 and (batch * seq) % OUT_ROWS == 0
    assert seq // GROUP < WORD_BITS
    k_top = min(TOPK_MAX, seq // 4)

    rel_bias = rel_bias.astype(F32)
    tb, tm = _bias_tables(rel_bias)
    lbraw = hgrn_lb_raw.astype(F32)
    h = x.reshape(batch * seq, D_MODEL).astype(F32)
    hm = meta_tokens.astype(F32)
    s_zero = jnp.zeros((HGRN_HEADS, HGRN_EXPAND, HGRN_EXPAND), F32)

    wp_all = jnp.concatenate(
        [w_in[:, :, :_RAW_SPLIT], jnp.zeros((DEPTH, D_MODEL, _P_GA - _RAW_SPLIT), w_in.dtype),
         w_in[:, :, _RAW_SPLIT:]], axis=2).astype(BF16)

    for l in range(DEPTH):
        wp = wp_all[l]
        wuk_bd = jnp.stack([_block_diag(w_uk[l][2 * p:2 * p + 2])
                            for p in range(ATTN_HEADS // 2)]).astype(BF16)
        wuvt = jnp.stack([_block_diag(w_uv[l][2 * p:2 * p + 2]).T
                          for p in range(ATTN_HEADS // 2)]).astype(BF16)
        wo = w_out[l].astype(BF16)
        kvg = kv_norm_g[l].reshape(1, KV_RANK).astype(F32)
        ng = hgrn_norm_g[l].reshape(1, HGRN_EXPAND).astype(F32)
        lng = ln_g[l].reshape(1, D_MODEL).astype(F32)
        lnb = ln_b[l].reshape(1, D_MODEL).astype(F32)

        hm_pad = jnp.pad(hm, ((0, QB - N_META), (0, 0)))
        (qlt_m, cb_m, cbt_m, _, _, _, ga_m, qf_m, g_m, kk_m, v_m, gh_m) = _project(
            l, hm_pad, wp, wuk_bd, kvg, lbraw, QB)
        cm = cb_m[:N_META]
        cmt = cbt_m[0][:, :N_META]
        a_m = _meta_attention(qlt_m[:, :, :N_META], cm, cmt[:KV_RANK], ga_m[:N_META], tb, wuvt)
        r_m, s_m = _hgrn(1, N_META, N_META, 1, qf_m[:N_META], g_m[:N_META], kk_m[:N_META],
                         v_m[:N_META], gh_m[:N_META], s_zero, ng)

        (qlt, cb, cbt, qit, kib, wt, ga, qf, g, kk, v, gh) = _project(
            l, h, wp, wuk_bd, kvg, lbraw, PROJ_ROWS)
        a = _attention(k_top, batch, seq, qit, kib, wt, qlt, cb, cbt, cm, cmt, ga, tb, tm, wuvt)
        r, _ = _hgrn(batch, seq, HGRN_CHUNK, HGRN_STEP_CHUNKS, qf, g, kk, v, gh, s_m[0], ng)

        hm = _out_project(a_m, r_m, hm, wo, lng, lnb, N_META)
        h = _out_project(a, r, h, wo, lng, lnb, OUT_ROWS)

    return h.reshape(batch, seq, D_MODEL).astype(x.dtype)
```

```python
import functools
import math

import numpy as np
import jax
import jax.numpy as jnp
from jax import lax
from jax.experimental import pallas as pl
from jax.experimental.pallas import tpu as pltpu

D_MODEL = 1024
DEPTH = 2
N_META = 16
ATTN_WIDTH = 512
HGRN_WIDTH = 512
ATTN_HEADS = 8
ATTN_HEAD_DIM = 64
KV_RANK = 128
IDX_HEADS = 4
IDX_DIM = 64
TOPK_MAX = 256
HGRN_EXPAND = 128
HGRN_HEADS = 4
REL_BUCKETS = 32
REL_MAX_DIST = 128
DN_ALPHA = (2 * DEPTH) ** 0.25
EPS = 1e-6
LOG2E = math.log2(math.e)

F32 = jnp.float32
BF16 = jnp.bfloat16

LANES = 128
SUBLANES = 8
QB = 128
KEY_TILES = 4
CK = KEY_TILES * QB
WORD_BITS = 32
GROUP = WORD_BITS * SUBLANES
PV_ROWS = KV_RANK + 16
HGRN_CHUNK = 128
HGRN_STEP_CHUNKS = 4
PROJ_ROWS = 512
OUT_ROWS = 1024
VMEM_LIMIT = 40 * 1024 * 1024

_P_QA, _P_CKV, _P_QI, _P_KW, _P_GA, _P_QH, _P_FH, _P_IH, _P_GH, _P_END = (
    0, 512, 640, 896, 1024, 1536, 2048, 2560, 3072, 3584)
_RAW_SPLIT = 964

NEG_INF_KEY = -2139095041
_WORD_INDEX_LOW = (0xFFFF0000, 0xFF00FF00, 0xF0F0F0F0, 0xCCCCCCCC, 0xAAAAAAAA)
_WORD_BIT = [int(np.uint32(1 << (31 - j)).astype(np.int32)) for j in range(32)]
INT_MIN = -2147483648


def _bucket_starts():
    max_exact = REL_BUCKETS // 2
    n = np.arange(0, 4 * REL_MAX_DIST)
    nf = np.maximum(n, 1).astype(np.float64)
    large = max_exact + (np.log(nf / max_exact) / math.log(REL_MAX_DIST / max_exact)
                         * (REL_BUCKETS - max_exact)).astype(np.int64)
    large = np.minimum(large, REL_BUCKETS - 1)
    bucket = np.where(n < max_exact, n, large)
    return [int(np.argmax(bucket >= b)) for b in range(REL_BUCKETS)]


_BUCKET_START = _bucket_starts()


def _dot(a, b):
    return jnp.dot(a, b, preferred_element_type=F32)


def _dot_nt(a, b):
    return lax.dot_general(a, b, (((1,), (1,)), ((), ())), preferred_element_type=F32)


def _dot_tn(a, b):
    return lax.dot_general(a, b, (((0,), (0,)), ((), ())), preferred_element_type=F32)


def _sigmoid(x):
    return 1.0 / (1.0 + jnp.exp(-x))


def _params(*sem):
    return pltpu.CompilerParams(dimension_semantics=sem, vmem_limit_bytes=VMEM_LIMIT)


def _fold_rows(x, op):
    parts = [x[r:r + SUBLANES] for r in range(0, x.shape[0], SUBLANES)]
    while len(parts) > 1:
        pairs = [op(parts[k], parts[k + 1]) for k in range(0, len(parts) - 1, 2)]
        parts = pairs + parts[len(parts) - len(parts) % 2:]
    return parts[0]


def _proj_kernel(layer, h_ref, wp_ref, wuk_ref, kvg_ref, lbraw_ref,
                 qlt_ref, cb_ref, cbt_ref, qit_ref, kib_ref, wt_ref, ga_ref,
                 qf_ref, g_ref, kk_ref, v_ref, gh_ref):
    rows = h_ref.shape[0]
    hb = h_ref[...].astype(BF16)

    def proj(lo, hi):
        return _dot(hb, wp_ref[:, lo:hi])

    qa = proj(_P_QA, _P_CKV).astype(BF16)
    pair_in = 2 * ATTN_HEAD_DIM
    ql = jnp.concatenate(
        [_dot(qa[:, p * pair_in:(p + 1) * pair_in], wuk_ref[p]) for p in range(ATTN_HEADS // 2)],
        axis=1) * (ATTN_HEAD_DIM ** -0.5 * LOG2E)
    qlt_ref[...] = ql.T.reshape(ATTN_HEADS, KV_RANK, rows).astype(BF16)

    ckv = proj(_P_CKV, _P_QI)
    c = ckv * lax.rsqrt(jnp.mean(ckv * ckv, axis=-1, keepdims=True) + EPS) * kvg_ref[...]
    cb_ref[...] = c.astype(BF16)
    ct = c.T
    ones_row = jnp.where(lax.broadcasted_iota(jnp.int32, (PV_ROWS - KV_RANK, QB), 0) == 0, 1.0, 0.0)
    for t in range(rows // QB):
        cbt_ref[t] = jnp.concatenate([ct[:, t * QB:(t + 1) * QB], ones_row], axis=0).astype(BF16)

    qit_ref[...] = proj(_P_QI, _P_KW).T.astype(BF16)
    kw = proj(_P_KW, _P_GA)
    kib_ref[...] = kw[:, :IDX_DIM].astype(BF16)
    wt_ref[...] = kw.T[IDX_DIM:IDX_DIM + SUBLANES, :] * (IDX_HEADS ** -0.5 * IDX_DIM ** -0.5)

    ga = proj(_P_GA, _P_QH)
    ga_ref[...] = ga * _sigmoid(ga)

    qh = proj(_P_QH, _P_FH)
    qf_ref[...] = qh * _sigmoid(qh) * (HGRN_EXPAND ** -0.5)

    raw = lbraw_ref[...]
    ex = jnp.exp(raw - jnp.max(raw, axis=0, keepdims=True))
    lbp = ex / jnp.sum(ex, axis=0, keepdims=True)
    lb = lbp[0:1]
    for i in range(1, layer + 1):
        lb = lb + lbp[i:i + 1]
    lb = lb - lbp[0:1]
    f = lb + (1.0 - lb) * _sigmoid(proj(_P_FH, _P_IH))
    g_ref[...] = jnp.log2(f)
    kk_ref[...] = 1.0 - f
    v_ref[...] = proj(_P_IH, _P_GH)
    gh = proj(_P_GH, _P_END)
    gh_ref[...] = gh * _sigmoid(gh)


def _project(layer, h2d, wp, wuk_bd, kvg, lbraw, rows):
    t = h2d.shape[0]
    grid = (t // rows,)
    row_spec = lambda w: pl.BlockSpec((rows, w), lambda r: (r, 0))
    col_spec = lambda n: pl.BlockSpec((n, rows), lambda r: (0, r))
    full = lambda a: pl.BlockSpec(a.shape, lambda r: (0,) * a.ndim)
    out_shape = (
        jax.ShapeDtypeStruct((ATTN_HEADS, KV_RANK, t), BF16),
        jax.ShapeDtypeStruct((t, KV_RANK), BF16),
        jax.ShapeDtypeStruct((t // QB, PV_ROWS, QB), BF16),
        jax.ShapeDtypeStruct((IDX_HEADS * IDX_DIM, t), BF16),
        jax.ShapeDtypeStruct((t, IDX_DIM), BF16),
        jax.ShapeDtypeStruct((SUBLANES, t), F32),
        jax.ShapeDtypeStruct((t, ATTN_WIDTH), F32),
        jax.ShapeDtypeStruct((t, HGRN_WIDTH), F32),
        jax.ShapeDtypeStruct((t, HGRN_WIDTH), F32),
        jax.ShapeDtypeStruct((t, HGRN_WIDTH), F32),
        jax.ShapeDtypeStruct((t, HGRN_WIDTH), F32),
        jax.ShapeDtypeStruct((t, HGRN_WIDTH), F32),
    )
    out_specs = (
        pl.BlockSpec((ATTN_HEADS, KV_RANK, rows), lambda r: (0, 0, r)),
        row_spec(KV_RANK),
        pl.BlockSpec((rows // QB, PV_ROWS, QB), lambda r: (r, 0, 0)),
        col_spec(IDX_HEADS * IDX_DIM), row_spec(IDX_DIM), col_spec(SUBLANES),
        row_spec(ATTN_WIDTH), row_spec(HGRN_WIDTH), row_spec(HGRN_WIDTH), row_spec(HGRN_WIDTH),
        row_spec(HGRN_WIDTH), row_spec(HGRN_WIDTH),
    )
    return pl.pallas_call(
        functools.partial(_proj_kernel, layer),
        grid=grid,
        in_specs=[row_spec(D_MODEL), full(wp), full(wuk_bd), full(kvg), full(lbraw)],
        out_specs=out_specs,
        out_shape=out_shape,
        compiler_params=_params("arbitrary"),
        name=f"proj_l{layer}_r{rows}",
    )(h2d, wp, wuk_bd, kvg, lbraw)


def _bias_tile(dist, rb_ref, h):
    out = jnp.full(dist.shape, rb_ref[0, h], F32)
    for b in range(1, REL_BUCKETS):
        out = jnp.where(dist >= _BUCKET_START[b], rb_ref[b, h], out)
    return out


def _bias_kernel(rb_ref, tb_ref, tm_ref):
    k = lax.broadcasted_iota(jnp.int32, (QB, QB), 0)
    q = lax.broadcasted_iota(jnp.int32, (QB, QB), 1)
    km = lax.broadcasted_iota(jnp.int32, (N_META, QB), 0)
    qm = lax.broadcasted_iota(jnp.int32, (N_META, QB), 1)
    for h in range(ATTN_HEADS):
        lanes = slice(h * QB, (h + 1) * QB)
        far = rb_ref[REL_BUCKETS - 1, h] * LOG2E
        tb_ref[0, :, lanes] = jnp.full((QB, QB), far, F32)
        tb_ref[1, :, lanes] = _bias_tile(q - k + QB, rb_ref, h) * LOG2E
        tb_ref[2, :, lanes] = _bias_tile(jnp.maximum(q - k, 0), rb_ref, h) * LOG2E
        tm_ref[0, :, lanes] = jnp.full((N_META, QB), far, F32)
        tm_ref[1, :, lanes] = _bias_tile(N_META + qm - km, rb_ref, h) * LOG2E


def _bias_tables(rel_bias):
    wide = ATTN_HEADS * QB
    return pl.pallas_call(
        _bias_kernel,
        in_specs=[pl.BlockSpec(memory_space=pltpu.SMEM)],
        out_shape=(jax.ShapeDtypeStruct((3, QB, wide), F32), jax.ShapeDtypeStruct((2, N_META, wide), F32)),
        name="bias_tables",
    )(rel_bias)


def _pair_project(wuvt_ref, o_all):
    rows = 2 * KV_RANK
    return jnp.concatenate(
        [_dot(wuvt_ref[p], o_all[p * rows:(p + 1) * rows]) for p in range(ATTN_HEADS // 2)], axis=0)


def _to_ukey(s):
    bits = pltpu.bitcast(s, jnp.int32)
    return bits ^ ((bits >> 31) | INT_MIN)


def _bit_transpose(x):
    x = list(x)
    j, m = 16, 0x0000FFFF
    while j:
        k = 0
        while k < WORD_BITS:
            t = (x[k] ^ lax.shift_right_logical(x[k + j], jnp.int32(j))) & m
            x[k] = x[k] ^ t
            x[k + j] = x[k + j] ^ jnp.left_shift(t, j)
            k = (k + j + 1) & ~j
        j >>= 1
        m = (m ^ (m << j)) & 0xFFFFFFFF
    return x


def _attn_kernel(k_top, qit_ref, kib_ref, wt_ref, qlt_ref, cb_ref, cbt_ref, cm_ref, cmt_ref, ga_ref,
                 tb_ref, tm_ref, wuvt_ref, a_ref,
                 s_scr, plane_scr, tie_scr, gt_scr, sel_scr, thr_scr, ngt_scr, am_scr, lg_scr, top_scr, mrun_scr, oacc_scr):
    i = pl.program_id(1)
    nch = i // KEY_TILES + 1
    wide = ATTN_HEADS * QB
    prows = plane_scr.shape[0] // WORD_BITS

    def rows_of(c):
        return pl.ds(pl.multiple_of(c * CK, CK), CK)

    def tile_rows(j):
        return pl.ds(pl.multiple_of(j * QB, QB), QB)

    def group_word(ref, g):
        return ref[pl.ds(pl.multiple_of(g * SUBLANES, SUBLANES), SUBLANES), :]

    qit = qit_ref[...]
    wt = wt_ref[...]
    k_loc = lax.broadcasted_iota(jnp.int32, (CK, QB), 0)
    q_pos = i * QB + lax.broadcasted_iota(jnp.int32, (CK, QB), 1)

    def score_chunk(c, last):
        kc = kib_ref[rows_of(c), :]
        s = jnp.maximum(_dot(kc, qit[0:IDX_DIM]), 0.0) * wt[0:1]
        for h in range(1, IDX_HEADS):
            s = s + jnp.maximum(_dot(kc, qit[h * IDX_DIM:(h + 1) * IDX_DIM]), 0.0) * wt[h:h + 1]
        if last:
            s = jnp.where(c * CK + k_loc <= q_pos, s, -jnp.inf)
        s_scr[rows_of(c), :] = s
        u = _to_ukey(s)
        for gl in range(CK // GROUP):
            words = [u[gl * GROUP + j * SUBLANES:gl * GROUP + (j + 1) * SUBLANES] for j in range(WORD_BITS)]
            g = c * (CK // GROUP) + gl
            for p, word in enumerate(_bit_transpose(words)):
                plane_scr[pl.ds(pl.multiple_of(p * prows + g * SUBLANES, SUBLANES), SUBLANES), :] = word

    def score_body(p, carry):
        score_chunk(2 * p, False)
        score_chunk(2 * p + 1, False)
        return carry

    lax.fori_loop(0, (nch - 1) // 2, score_body, 0)

    @pl.when((nch - 1) % 2 == 1)
    def _():
        score_chunk(nch - 2, False)

    score_chunk(nch - 1, True)

    kf = float(k_top)
    group = lax.broadcasted_iota(jnp.int32, (prows, QB), 0) // SUBLANES
    active = jnp.where(group < nch * (CK // GROUP), -1, 0)

    def ones_per_query(x):
        return jnp.sum(_fold_rows(lax.population_count(x), jnp.add).astype(F32), axis=0, keepdims=True)

    def from_ukey(u):
        return pltpu.bitcast(jnp.where(u < 0, u ^ INT_MIN, ~u), F32)

    def bit_body(p, carry):
        alive, want, thr = carry
        hi = alive & plane_scr[pl.ds(pl.multiple_of(p * prows, prows), prows), :]
        n_hi = ones_per_query(hi)
        up = n_hi >= want
        return (jnp.where(up, hi, alive ^ hi), jnp.where(up, want, want - n_hi),
                thr | jnp.where(up, jnp.left_shift(jnp.int32(1), WORD_BITS - 1 - p), 0))

    _, _, thr_bits = lax.fori_loop(
        0, WORD_BITS, bit_body,
        (active, jnp.full((1, QB), kf, F32), jnp.zeros((1, QB), jnp.int32)))

    def census(thr_f):
        def body(c, carry):
            s = s_scr[rows_of(c), :]
            for gl in range(CK // GROUP):
                rows = [s[gl * GROUP + j * SUBLANES:gl * GROUP + (j + 1) * SUBLANES] for j in range(WORD_BITS)]
                words = pl.ds(pl.multiple_of((c * (CK // GROUP) + gl) * SUBLANES, SUBLANES), SUBLANES)
                tie_scr[words, :] = _fold_rows(jnp.concatenate(
                    [jnp.where(r == thr_f, _WORD_BIT[j], 0) for j, r in enumerate(rows)], axis=0), jnp.bitwise_or)
                gt_scr[words, :] = _fold_rows(jnp.concatenate(
                    [jnp.where(r > thr_f, _WORD_BIT[j], 0) for j, r in enumerate(rows)], axis=0), jnp.bitwise_or)
            return carry
        lax.fori_loop(0, nch, body, 0)
        return ones_per_query(gt_scr[...] & active)

    def keep(thr_f, n_gt):
        thr_scr[...] = jnp.broadcast_to(thr_f, (SUBLANES, QB))
        ngt_scr[...] = jnp.broadcast_to(n_gt, (SUBLANES, QB))

    thr_fast = from_ukey(thr_bits)
    n_gt_fast = census(thr_fast)
    n_eq_fast = ones_per_query(tie_scr[...] & active)
    keep(thr_fast, n_gt_fast)
    unconfirmed = (n_gt_fast >= kf) | (n_gt_fast + n_eq_fast < kf)

    @pl.when(jnp.max(jnp.where(unconfirmed, 1.0, 0.0)) > 0.5)
    def _():
        def count_ge(cand_f):
            def body(c, acc):
                hit = jnp.where(s_scr[rows_of(c), :] >= cand_f, 1.0, 0.0)
                return acc + jnp.sum(hit.reshape(KEY_TILES, QB, QB), axis=0)
            acc = lax.fori_loop(0, nch, body, jnp.zeros((QB, QB), F32))
            return jnp.sum(acc, axis=0, keepdims=True)

        def slow_body(p, prefix):
            cand = prefix | jnp.left_shift(jnp.int32(1), WORD_BITS - 1 - p)
            floor = jnp.where(cand >= 0, jnp.maximum(cand, NEG_INF_KEY ^ INT_MIN), cand)
            return jnp.where(count_ge(from_ukey(floor)) >= kf, cand, prefix)

        thr_slow = from_ukey(lax.fori_loop(0, WORD_BITS, slow_body, jnp.zeros((1, QB), jnp.int32)))
        keep(thr_slow, census(thr_slow))

    thr_f = thr_scr[0:1, :]
    n_gt = ngt_scr[0:1, :]

    rest = jnp.where(thr_f == -jnp.inf, 0.0, kf - n_gt)
    alive = tie_scr[...] & active
    row = lax.broadcasted_iota(jnp.int32, (prows, QB), 0)
    group_bits = [1 << b for b in reversed(range((prows // SUBLANES - 1).bit_length()))]
    low_half = ([jnp.where((row & (SUBLANES * gb)) == 0, -1, 0) for gb in group_bits]
                + [int(np.uint32(m).astype(np.int32)) for m in _WORD_INDEX_LOW]
                + [jnp.where((row & rb) == 0, -1, 0) for rb in (4, 2, 1)])
    ties, taken = alive, jnp.zeros((prows, QB), jnp.int32)
    for low in low_half:
        lo = ties & low
        n_lo = ones_per_query(lo)
        inside = n_lo >= rest
        taken = taken | jnp.where(inside, 0, lo)
        ties = jnp.where(inside, lo, ties ^ lo)
        rest = jnp.where(inside, rest, rest - n_lo)
    sel_scr[...] = taken | jnp.where(rest >= 1.0, ties, 0)

    def mask_body(c, carry):
        for gl in range(CK // GROUP):
            g = c * (CK // GROUP) + gl
            word = group_word(sel_scr, g) | group_word(gt_scr, g)
            am_scr[pl.ds(pl.multiple_of(g * GROUP, GROUP), GROUP), :] = jnp.concatenate(
                [jnp.where(jnp.left_shift(word, j) < 0, 0.0, -jnp.inf) for j in range(WORD_BITS)], axis=0)
        return carry

    lax.fori_loop(0, nch, mask_body, 0)

    qt = jnp.concatenate([qlt_ref[h] for h in range(ATTN_HEADS)], axis=1)

    def make_logits(slot, c):
        lg = _dot(cb_ref[rows_of(c), :], qt)
        top = jnp.full((SUBLANES, wide), -jnp.inf, F32)
        for t in range(KEY_TILES):
            j = c * KEY_TILES + t
            kind = jnp.where(j == i, 2, jnp.where(j == i - 1, 1, 0))
            am = am_scr[tile_rows(j), :]
            lgt = lg[t * QB:(t + 1) * QB] + tb_ref[kind] + jnp.concatenate([am] * ATTN_HEADS, axis=1)
            lg_scr[slot, t * QB:(t + 1) * QB, :] = lgt
            top = jnp.maximum(top, _fold_rows(lgt, jnp.maximum))
        top_scr[slot] = top

    def absorb(slot, c):
        m_run = mrun_scr[0:1, :]
        m_new = jnp.maximum(m_run, jnp.max(top_scr[slot], axis=0, keepdims=True))
        pr = jnp.exp2(lg_scr[slot] - m_new)
        ct = cbt_ref[pl.ds(c * KEY_TILES, KEY_TILES)]
        ct = jnp.concatenate([ct[t] for t in range(KEY_TILES)], axis=1)
        oacc_scr[...] = oacc_scr[...] * jnp.exp2(m_run - m_new) + _dot(ct, pr.astype(BF16))
        mrun_scr[...] = jnp.broadcast_to(m_new, (SUBLANES, wide))

    lgm = _dot(cm_ref[...], qt) + tm_ref[jnp.where(i == 0, 1, 0)]
    m_meta = jnp.max(lgm, axis=0, keepdims=True)
    mrun_scr[...] = jnp.broadcast_to(m_meta, (SUBLANES, wide))
    oacc_scr[...] = _dot(cmt_ref[...], jnp.exp2(lgm - m_meta).astype(BF16))

    make_logits(0, 0)

    def pair_body(p, carry):
        make_logits(1, 2 * p + 1)
        absorb(0, 2 * p)
        make_logits(0, 2 * p + 2)
        absorb(1, 2 * p + 1)
        return carry

    pairs = (nch - 1) // 2
    lax.fori_loop(0, pairs, pair_body, 0)
    rest_two = nch - 2 * pairs == 2

    @pl.when(rest_two)
    def _():
        make_logits(1, nch - 1)

    absorb(0, 2 * pairs)

    @pl.when(rest_two)
    def _():
        absorb(1, nch - 1)

    o = (oacc_scr[:KV_RANK, :] / oacc_scr[KV_RANK:KV_RANK + 1, :]).astype(BF16)
    o_all = jnp.concatenate([o[:, h * QB:(h + 1) * QB] for h in range(ATTN_HEADS)], axis=0)
    a_t = _pair_project(wuvt_ref, o_all)
    a_ref[...] = (a_t.T * ga_ref[...]).astype(BF16)


def _attention(k_top, batch, seq, qit, kib, wt, qlt, cb, cbt, cm, cmt, ga, tb, tm, wuvt):
    nq = seq // QB
    full = lambda a: pl.BlockSpec(a.shape, lambda b, i: (0,) * a.ndim)
    qcols = lambda n: pl.BlockSpec((n, QB), lambda b, i: (0, b * nq + i))
    per_batch = lambda w: pl.BlockSpec((seq, w), lambda b, i: (b, 0))
    return pl.pallas_call(
        functools.partial(_attn_kernel, k_top),
        grid=(batch, nq),
        in_specs=[
            qcols(IDX_HEADS * IDX_DIM),
            per_batch(IDX_DIM),
            qcols(SUBLANES),
            pl.BlockSpec((ATTN_HEADS, KV_RANK, QB), lambda b, i: (0, 0, b * nq + i)),
            per_batch(KV_RANK),
            pl.BlockSpec((nq, PV_ROWS, QB), lambda b, i: (b, 0, 0)),
            full(cm), full(cmt),
            pl.BlockSpec((QB, ATTN_WIDTH), lambda b, i: (b * nq + i, 0)),
            full(tb), full(tm), full(wuvt),
        ],
        out_specs=pl.BlockSpec((QB, ATTN_WIDTH), lambda b, i: (b * nq + i, 0)),
        out_shape=jax.ShapeDtypeStruct((batch * seq, ATTN_WIDTH), BF16),
        scratch_shapes=[
            pltpu.VMEM((seq, QB), F32),
            pltpu.VMEM((seq, QB), jnp.int32),
            pltpu.VMEM((seq // WORD_BITS, QB), jnp.int32),
            pltpu.VMEM((seq // WORD_BITS, QB), jnp.int32),
            pltpu.VMEM((seq // WORD_BITS, QB), jnp.int32),
            pltpu.VMEM((SUBLANES, QB), F32),
            pltpu.VMEM((SUBLANES, QB), F32),
            pltpu.VMEM((seq, QB), F32),
            pltpu.VMEM((2, CK, ATTN_HEADS * QB), F32),
            pltpu.VMEM((2, SUBLANES, ATTN_HEADS * QB), F32),
            pltpu.VMEM((SUBLANES, ATTN_HEADS * QB), F32),
            pltpu.VMEM((PV_ROWS, ATTN_HEADS * QB), F32),
        ],
        compiler_params=_params("arbitrary", "arbitrary"),
        name="dsa_attention",
    )(qit, kib, wt, qlt, cb, cbt, cm, cmt, ga, tb, tm, wuvt)


def _meta_attn_kernel(qlt_ref, cm_ref, cmt_ref, ga_ref, tb_ref, wuvt_ref, a_ref, o_scr):
    cm = cm_ref[...]
    cmt = cmt_ref[...]
    k = lax.broadcasted_iota(jnp.int32, (N_META, N_META), 0)
    q = lax.broadcasted_iota(jnp.int32, (N_META, N_META), 1)
    for h in range(ATTN_HEADS):
        lg = _dot(cm, qlt_ref[h]) + tb_ref[2, :N_META, h * QB:h * QB + N_META]
        lg = jnp.where(k <= q, lg, -jnp.inf)
        pr = jnp.exp2(lg - jnp.max(lg, axis=0, keepdims=True))
        pr = pr / jnp.sum(pr, axis=0, keepdims=True)
        o_scr[h * KV_RANK:(h + 1) * KV_RANK, :] = _dot(cmt, pr.astype(BF16)).astype(BF16)
    a_ref[...] = (_pair_project(wuvt_ref, o_scr[...]).T * ga_ref[...]).astype(BF16)


def _meta_attention(qlt_m, cm, cmt, ga_m, tb, wuvt):
    return pl.pallas_call(
        _meta_attn_kernel,
        out_shape=jax.ShapeDtypeStruct((N_META, ATTN_WIDTH), BF16),
        scratch_shapes=[pltpu.VMEM((ATTN_HEADS * KV_RANK, N_META), BF16)],
        name="meta_attention",
    )(qlt_m, cm, cmt, ga_m, tb, wuvt)


def _hgrn_levels(chunk):
    return [1 << p for p in range(int(math.log2(chunk)))]


def _hgrn_masks(chunk):
    t = np.arange(chunk)[:, None]
    s = np.arange(chunk)[None, :]
    masks = [t == s]
    for m in _hgrn_levels(chunk):
        same = (t // (2 * m)) == (s // (2 * m))
        masks.append(same & ((t & m) != 0) & ((s & m) == 0))
    return np.stack(masks).astype(np.float32)


def _block_ref_rows(b, m, chunk, row):
    if 2 * m >= SUBLANES:
        pieces = []
        for blk in range(chunk // (2 * m)):
            r = blk * 2 * m + m - 1
            pieces.append(jnp.broadcast_to(b[r:r + 1, :], (2 * m, b.shape[1])))
        return pieces[0] if len(pieces) == 1 else jnp.concatenate(pieces, axis=0)
    delta = (row & (2 * m - 1)) - (m - 1)
    out = b
    for d in range(-(m - 1), m + 1):
        if d != 0:
            out = jnp.where(delta == d, pltpu.roll(b, d % chunk, 0), out)
    return out


def _split3(x):
    hi = x.astype(BF16)
    r1 = x - hi.astype(F32)
    mid = r1.astype(BF16)
    lo = (r1 - mid.astype(F32)).astype(BF16)
    return hi, mid, lo


def _hgrn_kernel(chunk, nsub, qf_ref, g_ref, kk_ref, v_ref, gh_ref, s0_ref, ng_ref, tri_ref, lm_ref,
                 r_ref, sfin_ref, st_scr):
    ci = pl.program_id(1)

    @pl.when(ci == 0)
    def _():
        st_scr[...] = s0_ref[...]

    tri = tri_ref[...]
    row = lax.broadcasted_iota(jnp.int32, (chunk, HGRN_EXPAND), 0)
    levels = _hgrn_levels(chunk)
    side = [jnp.where((row & m) != 0, 1.0, -1.0) for m in levels]
    for hd, sub in [(hd, sub) for hd in range(HGRN_HEADS) for sub in range(nsub)]:
        sl = (slice(sub * chunk, (sub + 1) * chunk), slice(hd * HGRN_EXPAND, (hd + 1) * HGRN_EXPAND))
        q = qf_ref[sl]
        g = g_ref[sl]
        k = kk_ref[sl]
        vb = v_ref[sl].astype(BF16)
        g_hi, g_mid, g_lo = _split3(g)
        b = _dot(tri, g_hi) + _dot(tri, g_mid) + _dot(tri, g_lo)

        a = _dot_nt(q.astype(BF16), k.astype(BF16)) * lm_ref[0]
        for li, m in enumerate(levels):
            e = jnp.exp2((b - _block_ref_rows(b, m, chunk, row)) * side[li])
            a = a + _dot_nt((q * e).astype(BF16), (k * e).astype(BF16)) * lm_ref[li + 1]

        st = st_scr[hd]
        o = _dot_nt((q * jnp.exp2(b)).astype(BF16), st.astype(BF16)) + _dot(a.astype(BF16), vb)
        b_last = b[chunk - 1:chunk, :]
        kd = (k * jnp.exp2(b_last - b)).astype(BF16)
        st_scr[hd] = st * jnp.exp2(b_last) + _dot_tn(vb, kd)

        rn = o * lax.rsqrt(jnp.mean(o * o, axis=-1, keepdims=True) + EPS) * ng_ref[...]
        r_ref[sl] = (rn * gh_ref[sl]).astype(BF16)

    @pl.when(ci == pl.num_programs(1) - 1)
    def _():
        sfin_ref[0] = st_scr[...]


def _hgrn(batch, seq, chunk, nsub, qf, g, kk, v, gh, s0, ng):
    nc = seq // (chunk * nsub)
    tri = jnp.asarray(np.tril(np.ones((chunk, chunk), np.float32)), BF16)
    lm = jnp.asarray(_hgrn_masks(chunk))
    blk = pl.BlockSpec((chunk * nsub, HGRN_WIDTH), lambda b, c: (b * nc + c, 0))
    full = lambda a: pl.BlockSpec(a.shape, lambda b, c: (0,) * a.ndim)
    return pl.pallas_call(
        functools.partial(_hgrn_kernel, chunk, nsub),
        grid=(batch, nc),
        in_specs=[blk, blk, blk, blk, blk, full(s0), full(ng), full(tri), full(lm)],
        out_specs=(blk, pl.BlockSpec((1, HGRN_HEADS, HGRN_EXPAND, HGRN_EXPAND),
                                     lambda b, c: (b, 0, 0, 0))),
        out_shape=(jax.ShapeDtypeStruct((batch * seq, HGRN_WIDTH), BF16),
                   jax.ShapeDtypeStruct((batch, HGRN_HEADS, HGRN_EXPAND, HGRN_EXPAND), F32)),
        scratch_shapes=[pltpu.VMEM((HGRN_HEADS, HGRN_EXPAND, HGRN_EXPAND), F32)],
        compiler_params=_params("arbitrary", "arbitrary"),
        name=f"hgrn2_c{chunk}x{nsub}",
    )(qf, g, kk, v, gh, s0, ng, tri, lm)


def _out_kernel(a_ref, r_ref, h_ref, wo_ref, lg_ref, lb_ref, o_ref):
    y = _dot(a_ref[...], wo_ref[:ATTN_WIDTH, :]) + _dot(r_ref[...], wo_ref[ATTN_WIDTH:, :])
    z = DN_ALPHA * h_ref[...] + y
    mu = jnp.mean(z, axis=-1, keepdims=True)
    zc = z - mu
    var = jnp.mean(zc * zc, axis=-1, keepdims=True)
    o_ref[...] = zc * lax.rsqrt(var + EPS) * lg_ref[...] + lb_ref[...]


def _out_project(a, r, h2d, wo, lng, lnb, rows):
    t = h2d.shape[0]
    row_spec = lambda w: pl.BlockSpec((rows, w), lambda i: (i, 0))
    full = lambda x: pl.BlockSpec(x.shape, lambda i: (0,) * x.ndim)
    return pl.pallas_call(
        _out_kernel,
        grid=(t // rows,),
        in_specs=[row_spec(ATTN_WIDTH), row_spec(HGRN_WIDTH), row_spec(D_MODEL),
                  full(wo), full(lng), full(lnb)],
        out_specs=row_spec(D_MODEL),
        out_shape=jax.ShapeDtypeStruct((t, D_MODEL), F32),
        compiler_params=_params("arbitrary"),
        name=f"out_proj_r{rows}",
    )(a, r, h2d, wo, lng, lnb)


def _block_diag(blocks):
    n, r, c = blocks.shape
    eye = jnp.eye(n, dtype=blocks.dtype)
    return (eye[:, None, :, None] * blocks[:, :, None, :]).reshape(n * r, n * c)


def kernel(x, meta_tokens, rel_bias, hgrn_lb_raw, w_in, kv_norm_g, w_uk, w_uv, hgrn_norm_g, w_out, ln_g, ln_b):
    batch, seq, _ = x.shape
    assert seq % CK == 0 and seq % (HGRN_CHUNK * HGRN_STEP_CHUNKS) == 0 and (batch * seq) % OUT_ROWS == 0
    assert seq // GROUP < WORD_BITS
    k_top = min(TOPK_MAX, seq // 4)

    rel_bias = rel_bias.astype(F32)
    tb, tm = _bias_tables(rel_bias)
    lbraw = hgrn_lb_raw.astype(F32)
    h = x.reshape(batch * seq, D_MODEL).astype(F32)
    hm = meta_tokens.astype(F32)
    s_zero = jnp.zeros((HGRN_HEADS, HGRN_EXPAND, HGRN_EXPAND), F32)

    wp_all = jnp.concatenate(
        [w_in[:, :, :_RAW_SPLIT], jnp.zeros((DEPTH, D_MODEL, _P_GA - _RAW_SPLIT), w_in.dtype),
         w_in[:, :, _RAW_SPLIT:]], axis=2).astype(BF16)

    for l in range(DEPTH):
        wp = wp_all[l]
        wuk_bd = jnp.stack([_block_diag(w_uk[l][2 * p:2 * p + 2])
                            for p in range(ATTN_HEADS // 2)]).astype(BF16)
        wuvt = jnp.stack([_block_diag(w_uv[l][2 * p:2 * p + 2]).T
                          for p in range(ATTN_HEADS // 2)]).astype(BF16)
        wo = w_out[l].astype(BF16)
        kvg = kv_norm_g[l].reshape(1, KV_RANK).astype(F32)
        ng = hgrn_norm_g[l].reshape(1, HGRN_EXPAND).astype(F32)
        lng = ln_g[l].reshape(1, D_MODEL).astype(F32)
        lnb = ln_b[l].reshape(1, D_MODEL).astype(F32)

        hm_pad = jnp.pad(hm, ((0, QB - N_META), (0, 0)))
        (qlt_m, cb_m, cbt_m, _, _, _, ga_m, qf_m, g_m, kk_m, v_m, gh_m) = _project(
            l, hm_pad, wp, wuk_bd, kvg, lbraw, QB)
        cm = cb_m[:N_META]
        cmt = cbt_m[0][:, :N_META]
        a_m = _meta_attention(qlt_m[:, :, :N_META], cm, cmt[:KV_RANK], ga_m[:N_META], tb, wuvt)
        r_m, s_m = _hgrn(1, N_META, N_META, 1, qf_m[:N_META], g_m[:N_META], kk_m[:N_META],
                         v_m[:N_META], gh_m[:N_META], s_zero, ng)

        (qlt, cb, cbt, qit, kib, wt, ga, qf, g, kk, v, gh) = _project(
            l, h, wp, wuk_bd, kvg, lbraw, PROJ_ROWS)
        a = _attention(k_top, batch, seq, qit, kib, wt, qlt, cb, cbt, cm, cmt, ga, tb, tm, wuvt)
        r, _ = _hgrn(batch, seq, HGRN_CHUNK, HGRN_STEP_CHUNKS, qf, g, kk, v, gh, s_m[0], ng)

        hm = _out_project(a_m, r_m, hm, wo, lng, lnb, N_META)
        h = _out_project(a, r, h, wo, lng, lnb, OUT_ROWS)

    return h.reshape(batch, seq, D_MODEL).astype(x.dtype)
```

```python
import functools
import math

import numpy as np
import jax
import jax.numpy as jnp
from jax import lax
from jax.experimental import pallas as pl
from jax.experimental.pallas import tpu as pltpu

D_MODEL = 1024
DEPTH = 2
N_META = 16
ATTN_WIDTH = 512
HGRN_WIDTH = 512
ATTN_HEADS = 8
ATTN_HEAD_DIM = 64
KV_RANK = 128
IDX_HEADS = 4
IDX_DIM = 64
TOPK_MAX = 256
HGRN_EXPAND = 128
HGRN_HEADS = 4
REL_BUCKETS = 32
REL_MAX_DIST = 128
DN_ALPHA = (2 * DEPTH) ** 0.25
EPS = 1e-6
LOG2E = math.log2(math.e)

F32 = jnp.float32
BF16 = jnp.bfloat16

LANES = 128
SUBLANES = 8
QB = 128
KEY_TILES = 4
CK = KEY_TILES * QB
WORD_BITS = 32
GROUP = WORD_BITS * SUBLANES
PV_ROWS = KV_RANK + 16
HGRN_CHUNK = 128
HGRN_STEP_CHUNKS = 4
PROJ_ROWS = 512
OUT_ROWS = 1024
VMEM_LIMIT = 40 * 1024 * 1024

_P_QA, _P_CKV, _P_QI, _P_KW, _P_GA, _P_QH, _P_FH, _P_IH, _P_GH, _P_END = (
    0, 512, 640, 896, 1024, 1536, 2048, 2560, 3072, 3584)
_RAW_SPLIT = 964

NEG_INF_KEY = -2139095041
_WORD_INDEX_LOW = (0xFFFF0000, 0xFF00FF00, 0xF0F0F0F0, 0xCCCCCCCC, 0xAAAAAAAA)
_WORD_BIT = [int(np.uint32(1 << (31 - j)).astype(np.int32)) for j in range(32)]
INT_MIN = -2147483648


def _bucket_starts():
    max_exact = REL_BUCKETS // 2
    n = np.arange(0, 4 * REL_MAX_DIST)
    nf = np.maximum(n, 1).astype(np.float64)
    large = max_exact + (np.log(nf / max_exact) / math.log(REL_MAX_DIST / max_exact)
                         * (REL_BUCKETS - max_exact)).astype(np.int64)
    large = np.minimum(large, REL_BUCKETS - 1)
    bucket = np.where(n < max_exact, n, large)
    return [int(np.argmax(bucket >= b)) for b in range(REL_BUCKETS)]


_BUCKET_START = _bucket_starts()


def _dot(a, b):
    return jnp.dot(a, b, preferred_element_type=F32)


def _dot_nt(a, b):
    return lax.dot_general(a, b, (((1,), (1,)), ((), ())), preferred_element_type=F32)


def _dot_tn(a, b):
    return lax.dot_general(a, b, (((0,), (0,)), ((), ())), preferred_element_type=F32)


def _sigmoid(x):
    return 1.0 / (1.0 + jnp.exp(-x))


def _params(*sem):
    return pltpu.CompilerParams(dimension_semantics=sem, vmem_limit_bytes=VMEM_LIMIT)


def _fold_rows(x, op):
    parts = [x[r:r + SUBLANES] for r in range(0, x.shape[0], SUBLANES)]
    while len(parts) > 1:
        pairs = [op(parts[k], parts[k + 1]) for k in range(0, len(parts) - 1, 2)]
        parts = pairs + parts[len(parts) - len(parts) % 2:]
    return parts[0]


def _proj_kernel(layer, h_ref, wp_ref, wuk_ref, kvg_ref, lbraw_ref,
                 qlt_ref, cb_ref, cbt_ref, qit_ref, kib_ref, wt_ref, ga_ref,
                 qf_ref, g_ref, kk_ref, v_ref, gh_ref):
    rows = h_ref.shape[0]
    hb = h_ref[...].astype(BF16)

    def proj(lo, hi):
        return _dot(hb, wp_ref[:, lo:hi])

    qa = proj(_P_QA, _P_CKV).astype(BF16)
    pair_in = 2 * ATTN_HEAD_DIM
    ql = jnp.concatenate(
        [_dot(qa[:, p * pair_in:(p + 1) * pair_in], wuk_ref[p]) for p in range(ATTN_HEADS // 2)],
        axis=1) * (ATTN_HEAD_DIM ** -0.5 * LOG2E)
    qlt_ref[...] = ql.T.reshape(ATTN_HEADS, KV_RANK, rows).astype(BF16)

    ckv = proj(_P_CKV, _P_QI)
    c = ckv * lax.rsqrt(jnp.mean(ckv * ckv, axis=-1, keepdims=True) + EPS) * kvg_ref[...]
    cb_ref[...] = c.astype(BF16)
    ct = c.T
    ones_row = jnp.where(lax.broadcasted_iota(jnp.int32, (PV_ROWS - KV_RANK, QB), 0) == 0, 1.0, 0.0)
    for t in range(rows // QB):
        cbt_ref[t] = jnp.concatenate([ct[:, t * QB:(t + 1) * QB], ones_row], axis=0).astype(BF16)

    qit_ref[...] = proj(_P_QI, _P_KW).T.astype(BF16)
    kw = proj(_P_KW, _P_GA)
    kib_ref[...] = kw[:, :IDX_DIM].astype(BF16)
    wt_ref[...] = kw.T[IDX_DIM:IDX_DIM + SUBLANES, :] * (IDX_HEADS ** -0.5 * IDX_DIM ** -0.5)

    ga = proj(_P_GA, _P_QH)
    ga_ref[...] = ga * _sigmoid(ga)

    qh = proj(_P_QH, _P_FH)
    qf_ref[...] = qh * _sigmoid(qh) * (HGRN_EXPAND ** -0.5)

    raw = lbraw_ref[...]
    ex = jnp.exp(raw - jnp.max(raw, axis=0, keepdims=True))
    lbp = ex / jnp.sum(ex, axis=0, keepdims=True)
    lb = lbp[0:1]
    for i in range(1, layer + 1):
        lb = lb + lbp[i:i + 1]
    lb = lb - lbp[0:1]
    f = lb + (1.0 - lb) * _sigmoid(proj(_P_FH, _P_IH))
    g_ref[...] = jnp.log2(f)
    kk_ref[...] = 1.0 - f
    v_ref[...] = proj(_P_IH, _P_GH)
    gh = proj(_P_GH, _P_END)
    gh_ref[...] = gh * _sigmoid(gh)


def _project(layer, h2d, wp, wuk_bd, kvg, lbraw, rows):
    t = h2d.shape[0]
    grid = (t // rows,)
    row_spec = lambda w: pl.BlockSpec((rows, w), lambda r: (r, 0))
    col_spec = lambda n: pl.BlockSpec((n, rows), lambda r: (0, r))
    full = lambda a: pl.BlockSpec(a.shape, lambda r: (0,) * a.ndim)
    out_shape = (
        jax.ShapeDtypeStruct((ATTN_HEADS, KV_RANK, t), BF16),
        jax.ShapeDtypeStruct((t, KV_RANK), BF16),
        jax.ShapeDtypeStruct((t // QB, PV_ROWS, QB), BF16),
        jax.ShapeDtypeStruct((IDX_HEADS * IDX_DIM, t), BF16),
        jax.ShapeDtypeStruct((t, IDX_DIM), BF16),
        jax.ShapeDtypeStruct((SUBLANES, t), F32),
        jax.ShapeDtypeStruct((t, ATTN_WIDTH), F32),
        jax.ShapeDtypeStruct((t, HGRN_WIDTH), F32),
        jax.ShapeDtypeStruct((t, HGRN_WIDTH), F32),
        jax.ShapeDtypeStruct((t, HGRN_WIDTH), F32),
        jax.ShapeDtypeStruct((t, HGRN_WIDTH), F32),
        jax.ShapeDtypeStruct((t, HGRN_WIDTH), F32),
    )
    out_specs = (
        pl.BlockSpec((ATTN_HEADS, KV_RANK, rows), lambda r: (0, 0, r)),
        row_spec(KV_RANK),
        pl.BlockSpec((rows // QB, PV_ROWS, QB), lambda r: (r, 0, 0)),
        col_spec(IDX_HEADS * IDX_DIM), row_spec(IDX_DIM), col_spec(SUBLANES),
        row_spec(ATTN_WIDTH), row_spec(HGRN_WIDTH), row_spec(HGRN_WIDTH), row_spec(HGRN_WIDTH),
        row_spec(HGRN_WIDTH), row_spec(HGRN_WIDTH),
    )
    return pl.pallas_call(
        functools.partial(_proj_kernel, layer),
        grid=grid,
        in_specs=[row_spec(D_MODEL), full(wp), full(wuk_bd), full(kvg), full(lbraw)],
        out_specs=out_specs,
        out_shape=out_shape,
        compiler_params=_params("arbitrary"),
        name=f"proj_l{layer}_r{rows}",
    )(h2d, wp, wuk_bd, kvg, lbraw)


def _bias_tile(dist, rb_ref, h):
    out = jnp.full(dist.shape, rb_ref[0, h], F32)
    for b in range(1, REL_BUCKETS):
        out = jnp.where(dist >= _BUCKET_START[b], rb_ref[b, h], out)
    return out


def _bias_kernel(rb_ref, tb_ref, tm_ref):
    k = lax.broadcasted_iota(jnp.int32, (QB, QB), 0)
    q = lax.broadcasted_iota(jnp.int32, (QB, QB), 1)
    km = lax.broadcasted_iota(jnp.int32, (N_META, QB), 0)
    qm = lax.broadcasted_iota(jnp.int32, (N_META, QB), 1)
    for h in range(ATTN_HEADS):
        lanes = slice(h * QB, (h + 1) * QB)
        far = rb_ref[REL_BUCKETS - 1, h] * LOG2E
        tb_ref[0, :, lanes] = jnp.full((QB, QB), far, F32)
        tb_ref[1, :, lanes] = _bias_tile(q - k + QB, rb_ref, h) * LOG2E
        tb_ref[2, :, lanes] = _bias_tile(jnp.maximum(q - k, 0), rb_ref, h) * LOG2E
        tm_ref[0, :, lanes] = jnp.full((N_META, QB), far, F32)
        tm_ref[1, :, lanes] = _bias_tile(N_META + qm - km, rb_ref, h) * LOG2E


def _bias_tables(rel_bias):
    wide = ATTN_HEADS * QB
    return pl.pallas_call(
        _bias_kernel,
        in_specs=[pl.BlockSpec(memory_space=pltpu.SMEM)],
        out_shape=(jax.ShapeDtypeStruct((3, QB, wide), F32), jax.ShapeDtypeStruct((2, N_META, wide), F32)),
        name="bias_tables",
    )(rel_bias)


def _pair_project(wuvt_ref, o_all):
    rows = 2 * KV_RANK
    return jnp.concatenate(
        [_dot(wuvt_ref[p], o_all[p * rows:(p + 1) * rows]) for p in range(ATTN_HEADS // 2)], axis=0)


def _to_ukey(s):
    bits = pltpu.bitcast(s, jnp.int32)
    return bits ^ ((bits >> 31) | INT_MIN)


def _bit_transpose(x):
    x = list(x)
    j, m = 16, 0x0000FFFF
    while j:
        k = 0
        while k < WORD_BITS:
            t = (x[k] ^ lax.shift_right_logical(x[k + j], jnp.int32(j))) & m
            x[k] = x[k] ^ t
            x[k + j] = x[k + j] ^ jnp.left_shift(t, j)
            k = (k + j + 1) & ~j
        j >>= 1
        m = (m ^ (m << j)) & 0xFFFFFFFF
    return x


def _attn_kernel(k_top, qit_ref, kib_ref, wt_ref, qlt_ref, cb_ref, cbt_ref, cm_ref, cmt_ref, ga_ref,
                 tb_ref, tm_ref, wuvt_ref, a_ref,
                 s_scr, plane_scr, tie_scr, gt_scr, sel_scr, thr_scr, ngt_scr, am_scr, lg_scr, top_scr, mrun_scr, oacc_scr):
    i = pl.program_id(1)
    nch = i // KEY_TILES + 1
    wide = ATTN_HEADS * QB
    prows = plane_scr.shape[0] // WORD_BITS

    def rows_of(c):
        return pl.ds(pl.multiple_of(c * CK, CK), CK)

    def tile_rows(j):
        return pl.ds(pl.multiple_of(j * QB, QB), QB)

    def group_word(ref, g):
        return ref[pl.ds(pl.multiple_of(g * SUBLANES, SUBLANES), SUBLANES), :]

    qit = qit_ref[...]
    wt = wt_ref[...]
    k_loc = lax.broadcasted_iota(jnp.int32, (CK, QB), 0)
    q_pos = i * QB + lax.broadcasted_iota(jnp.int32, (CK, QB), 1)

    def score_chunk(c, last):
        kc = kib_ref[rows_of(c), :]
        s = jnp.maximum(_dot(kc, qit[0:IDX_DIM]), 0.0) * wt[0:1]
        for h in range(1, IDX_HEADS):
            s = s + jnp.maximum(_dot(kc, qit[h * IDX_DIM:(h + 1) * IDX_DIM]), 0.0) * wt[h:h + 1]
        if last:
            s = jnp.where(c * CK + k_loc <= q_pos, s, -jnp.inf)
        s_scr[rows_of(c), :] = s
        u = _to_ukey(s)
        for gl in range(CK // GROUP):
            words = [u[gl * GROUP + j * SUBLANES:gl * GROUP + (j + 1) * SUBLANES] for j in range(WORD_BITS)]
            g = c * (CK // GROUP) + gl
            for p, word in enumerate(_bit_transpose(words)):
                plane_scr[pl.ds(pl.multiple_of(p * prows + g * SUBLANES, SUBLANES), SUBLANES), :] = word

    def score_body(p, carry):
        score_chunk(2 * p, False)
        score_chunk(2 * p + 1, False)
        return carry

    lax.fori_loop(0, (nch - 1) // 2, score_body, 0)

    @pl.when((nch - 1) % 2 == 1)
    def _():
        score_chunk(nch - 2, False)

    score_chunk(nch - 1, True)

    kf = float(k_top)
    group = lax.broadcasted_iota(jnp.int32, (prows, QB), 0) // SUBLANES
    active = jnp.where(group < nch * (CK // GROUP), -1, 0)

    def ones_per_query(x):
        return jnp.sum(_fold_rows(lax.population_count(x), jnp.add).astype(F32), axis=0, keepdims=True)

    def from_ukey(u):
        return pltpu.bitcast(jnp.where(u < 0, u ^ INT_MIN, ~u), F32)

    def bit_body(p, carry):
        alive, want, thr = carry
        hi = alive & plane_scr[pl.ds(pl.multiple_of(p * prows, prows), prows), :]
        n_hi = ones_per_query(hi)
        up = n_hi >= want
        return (jnp.where(up, hi, alive ^ hi), jnp.where(up, want, want - n_hi),
                thr | jnp.where(up, jnp.left_shift(jnp.int32(1), WORD_BITS - 1 - p), 0))

    _, _, thr_bits = lax.fori_loop(
        0, WORD_BITS, bit_body,
        (active, jnp.full((1, QB), kf, F32), jnp.zeros((1, QB), jnp.int32)))

    def census(thr_f):
        def body(c, carry):
            s = s_scr[rows_of(c), :]
            for gl in range(CK // GROUP):
                rows = [s[gl * GROUP + j * SUBLANES:gl * GROUP + (j + 1) * SUBLANES] for j in range(WORD_BITS)]
                words = pl.ds(pl.multiple_of((c * (CK // GROUP) + gl) * SUBLANES, SUBLANES), SUBLANES)
                tie_scr[words, :] = _fold_rows(jnp.concatenate(
                    [jnp.where(r == thr_f, _WORD_BIT[j], 0) for j, r in enumerate(rows)], axis=0), jnp.bitwise_or)
                gt_scr[words, :] = _fold_rows(jnp.concatenate(
                    [jnp.where(r > thr_f, _WORD_BIT[j], 0) for j, r in enumerate(rows)], axis=0), jnp.bitwise_or)
            return carry
        lax.fori_loop(0, nch, body, 0)
        return ones_per_query(gt_scr[...] & active)

    def keep(thr_f, n_gt):
        thr_scr[...] = jnp.broadcast_to(thr_f, (SUBLANES, QB))
        ngt_scr[...] = jnp.broadcast_to(n_gt, (SUBLANES, QB))

    thr_fast = from_ukey(thr_bits)
    n_gt_fast = census(thr_fast)
    n_eq_fast = ones_per_query(tie_scr[...] & active)
    keep(thr_fast, n_gt_fast)
    unconfirmed = (n_gt_fast >= kf) | (n_gt_fast + n_eq_fast < kf)

    @pl.when(jnp.max(jnp.where(unconfirmed, 1.0, 0.0)) > 0.5)
    def _():
        def count_ge(cand_f):
            def body(c, acc):
                hit = jnp.where(s_scr[rows_of(c), :] >= cand_f, 1.0, 0.0)
                return acc + jnp.sum(hit.reshape(KEY_TILES, QB, QB), axis=0)
            acc = lax.fori_loop(0, nch, body, jnp.zeros((QB, QB), F32))
            return jnp.sum(acc, axis=0, keepdims=True)

        def slow_body(p, prefix):
            cand = prefix | jnp.left_shift(jnp.int32(1), WORD_BITS - 1 - p)
            floor = jnp.where(cand >= 0, jnp.maximum(cand, NEG_INF_KEY ^ INT_MIN), cand)
            return jnp.where(count_ge(from_ukey(floor)) >= kf, cand, prefix)

        thr_slow = from_ukey(lax.fori_loop(0, WORD_BITS, slow_body, jnp.zeros((1, QB), jnp.int32)))
        keep(thr_slow, census(thr_slow))

    thr_f = thr_scr[0:1, :]
    n_gt = ngt_scr[0:1, :]

    rest = jnp.where(thr_f == -jnp.inf, 0.0, kf - n_gt)
    alive = tie_scr[...] & active
    row = lax.broadcasted_iota(jnp.int32, (prows, QB), 0)
    group_bits = [1 << b for b in reversed(range((prows // SUBLANES - 1).bit_length()))]
    low_half = ([jnp.where((row & (SUBLANES * gb)) == 0, -1, 0) for gb in group_bits]
                + [int(np.uint32(m).astype(np.int32)) for m in _WORD_INDEX_LOW]
                + [jnp.where((row & rb) == 0, -1, 0) for rb in (4, 2, 1)])
    ties, taken = alive, jnp.zeros((prows, QB), jnp.int32)
    for low in low_half:
        lo = ties & low
        n_lo = ones_per_query(lo)
        inside = n_lo >= rest
        taken = taken | jnp.where(inside, 0, lo)
        ties = jnp.where(inside, lo, ties ^ lo)
        rest = jnp.where(inside, rest, rest - n_lo)
    sel_scr[...] = taken | jnp.where(rest >= 1.0, ties, 0)

    def mask_body(c, carry):
        for gl in range(CK // GROUP):
            g = c * (CK // GROUP) + gl
            word = group_word(sel_scr, g) | group_word(gt_scr, g)
            am_scr[pl.ds(pl.multiple_of(g * GROUP, GROUP), GROUP), :] = jnp.concatenate(
                [jnp.where(jnp.left_shift(word, j) < 0, 0.0, -jnp.inf) for j in range(WORD_BITS)], axis=0)
        return carry

    lax.fori_loop(0, nch, mask_body, 0)

    qt = jnp.concatenate([qlt_ref[h] for h in range(ATTN_HEADS)], axis=1)

    halves = [slice(0, wide // 2), slice(wide // 2, wide)]

    def make_logits(slot, c):
        kc = cb_ref[rows_of(c), :]
        for lanes in halves:
            lg = _dot(kc, qt[:, lanes])
            top = jnp.full((SUBLANES, wide // 2), -jnp.inf, F32)
            for t in range(KEY_TILES):
                j = c * KEY_TILES + t
                kind = jnp.where(j == i, 2, jnp.where(j == i - 1, 1, 0))
                am = am_scr[tile_rows(j), :]
                lgt = (lg[t * QB:(t + 1) * QB] + tb_ref[kind, :, lanes]
                       + jnp.concatenate([am] * (ATTN_HEADS // 2), axis=1))
                lg_scr[slot, t * QB:(t + 1) * QB, lanes] = lgt
                top = jnp.maximum(top, _fold_rows(lgt, jnp.maximum))
            top_scr[slot, :, lanes] = top

    def absorb(slot, c):
        ct = cbt_ref[pl.ds(c * KEY_TILES, KEY_TILES)]
        ct = jnp.concatenate([ct[t] for t in range(KEY_TILES)], axis=1)
        for lanes in halves:
            m_run = mrun_scr[0:1, lanes]
            m_new = jnp.maximum(m_run, jnp.max(top_scr[slot, :, lanes], axis=0, keepdims=True))
            pr = jnp.exp2(lg_scr[slot, :, lanes] - m_new)
            oacc_scr[:, lanes] = oacc_scr[:, lanes] * jnp.exp2(m_run - m_new) + _dot(ct, pr.astype(BF16))
            mrun_scr[:, lanes] = jnp.broadcast_to(m_new, (SUBLANES, wide // 2))

    lgm = _dot(cm_ref[...], qt) + tm_ref[jnp.where(i == 0, 1, 0)]
    m_meta = jnp.max(lgm, axis=0, keepdims=True)
    mrun_scr[...] = jnp.broadcast_to(m_meta, (SUBLANES, wide))
    oacc_scr[...] = _dot(cmt_ref[...], jnp.exp2(lgm - m_meta).astype(BF16))

    make_logits(0, 0)

    def pair_body(p, carry):
        make_logits(1, 2 * p + 1)
        absorb(0, 2 * p)
        make_logits(0, 2 * p + 2)
        absorb(1, 2 * p + 1)
        return carry

    pairs = (nch - 1) // 2
    lax.fori_loop(0, pairs, pair_body, 0)
    rest_two = nch - 2 * pairs == 2

    @pl.when(rest_two)
    def _():
        make_logits(1, nch - 1)

    absorb(0, 2 * pairs)

    @pl.when(rest_two)
    def _():
        absorb(1, nch - 1)

    o = (oacc_scr[:KV_RANK, :] / oacc_scr[KV_RANK:KV_RANK + 1, :]).astype(BF16)
    o_all = jnp.concatenate([o[:, h * QB:(h + 1) * QB] for h in range(ATTN_HEADS)], axis=0)
    a_t = _pair_project(wuvt_ref, o_all)
    a_ref[...] = (a_t.T * ga_ref[...]).astype(BF16)


def _attention(k_top, batch, seq, qit, kib, wt, qlt, cb, cbt, cm, cmt, ga, tb, tm, wuvt):
    nq = seq // QB
    full = lambda a: pl.BlockSpec(a.shape, lambda b, i: (0,) * a.ndim)
    qcols = lambda n: pl.BlockSpec((n, QB), lambda b, i: (0, b * nq + i))
    per_batch = lambda w: pl.BlockSpec((seq, w), lambda b, i: (b, 0))
    return pl.pallas_call(
        functools.partial(_attn_kernel, k_top),
        grid=(batch, nq),
        in_specs=[
            qcols(IDX_HEADS * IDX_DIM),
            per_batch(IDX_DIM),
            qcols(SUBLANES),
            pl.BlockSpec((ATTN_HEADS, KV_RANK, QB), lambda b, i: (0, 0, b * nq + i)),
            per_batch(KV_RANK),
            pl.BlockSpec((nq, PV_ROWS, QB), lambda b, i: (b, 0, 0)),
            full(cm), full(cmt),
            pl.BlockSpec((QB, ATTN_WIDTH), lambda b, i: (b * nq + i, 0)),
            full(tb), full(tm), full(wuvt),
        ],
        out_specs=pl.BlockSpec((QB, ATTN_WIDTH), lambda b, i: (b * nq + i, 0)),
        out_shape=jax.ShapeDtypeStruct((batch * seq, ATTN_WIDTH), BF16),
        scratch_shapes=[
            pltpu.VMEM((seq, QB), F32),
            pltpu.VMEM((seq, QB), jnp.int32),
            pltpu.VMEM((seq // WORD_BITS, QB), jnp.int32),
            pltpu.VMEM((seq // WORD_BITS, QB), jnp.int32),
            pltpu.VMEM((seq // WORD_BITS, QB), jnp.int32),
            pltpu.VMEM((SUBLANES, QB), F32),
            pltpu.VMEM((SUBLANES, QB), F32),
            pltpu.VMEM((seq, QB), F32),
            pltpu.VMEM((2, CK, ATTN_HEADS * QB), F32),
            pltpu.VMEM((2, SUBLANES, ATTN_HEADS * QB), F32),
            pltpu.VMEM((SUBLANES, ATTN_HEADS * QB), F32),
            pltpu.VMEM((PV_ROWS, ATTN_HEADS * QB), F32),
        ],
        compiler_params=_params("arbitrary", "arbitrary"),
        name="dsa_attention",
    )(qit, kib, wt, qlt, cb, cbt, cm, cmt, ga, tb, tm, wuvt)


def _meta_attn_kernel(qlt_ref, cm_ref, cmt_ref, ga_ref, tb_ref, wuvt_ref, a_ref, o_scr):
    cm = cm_ref[...]
    cmt = cmt_ref[...]
    k = lax.broadcasted_iota(jnp.int32, (N_META, N_META), 0)
    q = lax.broadcasted_iota(jnp.int32, (N_META, N_META), 1)
    for h in range(ATTN_HEADS):
        lg = _dot(cm, qlt_ref[h]) + tb_ref[2, :N_META, h * QB:h * QB + N_META]
        lg = jnp.where(k <= q, lg, -jnp.inf)
        pr = jnp.exp2(lg - jnp.max(lg, axis=0, keepdims=True))
        pr = pr / jnp.sum(pr, axis=0, keepdims=True)
        o_scr[h * KV_RANK:(h + 1) * KV_RANK, :] = _dot(cmt, pr.astype(BF16)).astype(BF16)
    a_ref[...] = (_pair_project(wuvt_ref, o_scr[...]).T * ga_ref[...]).astype(BF16)


def _meta_attention(qlt_m, cm, cmt, ga_m, tb, wuvt):
    return pl.pallas_call(
        _meta_attn_kernel,
        out_shape=jax.ShapeDtypeStruct((N_META, ATTN_WIDTH), BF16),
        scratch_shapes=[pltpu.VMEM((ATTN_HEADS * KV_RANK, N_META), BF16)],
        name="meta_attention",
    )(qlt_m, cm, cmt, ga_m, tb, wuvt)


def _hgrn_levels(chunk):
    return [1 << p for p in range(int(math.log2(chunk)))]


def _hgrn_masks(chunk):
    t = np.arange(chunk)[:, None]
    s = np.arange(chunk)[None, :]
    masks = [t == s]
    for m in _hgrn_levels(chunk):
        same = (t // (2 * m)) == (s // (2 * m))
        masks.append(same & ((t & m) != 0) & ((s & m) == 0))
    return np.stack(masks).astype(np.float32)


def _block_ref_rows(b, m, chunk, row):
    if 2 * m >= SUBLANES:
        pieces = []
        for blk in range(chunk // (2 * m)):
            r = blk * 2 * m + m - 1
            pieces.append(jnp.broadcast_to(b[r:r + 1, :], (2 * m, b.shape[1])))
        return pieces[0] if len(pieces) == 1 else jnp.concatenate(pieces, axis=0)
    delta = (row & (2 * m - 1)) - (m - 1)
    out = b
    for d in range(-(m - 1), m + 1):
        if d != 0:
            out = jnp.where(delta == d, pltpu.roll(b, d % chunk, 0), out)
    return out


def _split3(x):
    hi = x.astype(BF16)
    r1 = x - hi.astype(F32)
    mid = r1.astype(BF16)
    lo = (r1 - mid.astype(F32)).astype(BF16)
    return hi, mid, lo


def _hgrn_kernel(chunk, nsub, qf_ref, g_ref, kk_ref, v_ref, gh_ref, s0_ref, ng_ref, tri_ref, lm_ref,
                 r_ref, sfin_ref, st_scr):
    ci = pl.program_id(1)

    @pl.when(ci == 0)
    def _():
        st_scr[...] = s0_ref[...]

    tri = tri_ref[...]
    row = lax.broadcasted_iota(jnp.int32, (chunk, HGRN_EXPAND), 0)
    levels = _hgrn_levels(chunk)
    side = [jnp.where((row & m) != 0, 1.0, -1.0) for m in levels]
    for hd, sub in [(hd, sub) for hd in range(HGRN_HEADS) for sub in range(nsub)]:
        sl = (slice(sub * chunk, (sub + 1) * chunk), slice(hd * HGRN_EXPAND, (hd + 1) * HGRN_EXPAND))
        q = qf_ref[sl]
        g = g_ref[sl]
        k = kk_ref[sl]
        vb = v_ref[sl].astype(BF16)
        g_hi, g_mid, g_lo = _split3(g)
        b = _dot(tri, g_hi) + _dot(tri, g_mid) + _dot(tri, g_lo)

        a = _dot_nt(q.astype(BF16), k.astype(BF16)) * lm_ref[0]
        for li, m in enumerate(levels):
            e = jnp.exp2((b - _block_ref_rows(b, m, chunk, row)) * side[li])
            a = a + _dot_nt((q * e).astype(BF16), (k * e).astype(BF16)) * lm_ref[li + 1]

        st = st_scr[hd]
        o = _dot_nt((q * jnp.exp2(b)).astype(BF16), st.astype(BF16)) + _dot(a.astype(BF16), vb)
        b_last = b[chunk - 1:chunk, :]
        kd = (k * jnp.exp2(b_last - b)).astype(BF16)
        st_scr[hd] = st * jnp.exp2(b_last) + _dot_tn(vb, kd)

        rn = o * lax.rsqrt(jnp.mean(o * o, axis=-1, keepdims=True) + EPS) * ng_ref[...]
        r_ref[sl] = (rn * gh_ref[sl]).astype(BF16)

    @pl.when(ci == pl.num_programs(1) - 1)
    def _():
        sfin_ref[0] = st_scr[...]


def _hgrn(batch, seq, chunk, nsub, qf, g, kk, v, gh, s0, ng):
    nc = seq // (chunk * nsub)
    tri = jnp.asarray(np.tril(np.ones((chunk, chunk), np.float32)), BF16)
    lm = jnp.asarray(_hgrn_masks(chunk))
    blk = pl.BlockSpec((chunk * nsub, HGRN_WIDTH), lambda b, c: (b * nc + c, 0))
    full = lambda a: pl.BlockSpec(a.shape, lambda b, c: (0,) * a.ndim)
    return pl.pallas_call(
        functools.partial(_hgrn_kernel, chunk, nsub),
        grid=(batch, nc),
        in_specs=[blk, blk, blk, blk, blk, full(s0), full(ng), full(tri), full(lm)],
        out_specs=(blk, pl.BlockSpec((1, HGRN_HEADS, HGRN_EXPAND, HGRN_EXPAND),
                                     lambda b, c: (b, 0, 0, 0))),
        out_shape=(jax.ShapeDtypeStruct((batch * seq, HGRN_WIDTH), BF16),
                   jax.ShapeDtypeStruct((batch, HGRN_HEADS, HGRN_EXPAND, HGRN_EXPAND), F32)),
        scratch_shapes=[pltpu.VMEM((HGRN_HEADS, HGRN_EXPAND, HGRN_EXPAND), F32)],
        compiler_params=_params("arbitrary", "arbitrary"),
        name=f"hgrn2_c{chunk}x{nsub}",
    )(qf, g, kk, v, gh, s0, ng, tri, lm)


def _out_kernel(a_ref, r_ref, h_ref, wo_ref, lg_ref, lb_ref, o_ref):
    y = _dot(a_ref[...], wo_ref[:ATTN_WIDTH, :]) + _dot(r_ref[...], wo_ref[ATTN_WIDTH:, :])
    z = DN_ALPHA * h_ref[...] + y
    mu = jnp.mean(z, axis=-1, keepdims=True)
    zc = z - mu
    var = jnp.mean(zc * zc, axis=-1, keepdims=True)
    o_ref[...] = zc * lax.rsqrt(var + EPS) * lg_ref[...] + lb_ref[...]


def _out_project(a, r, h2d, wo, lng, lnb, rows):
    t = h2d.shape[0]
    row_spec = lambda w: pl.BlockSpec((rows, w), lambda i: (i, 0))
    full = lambda x: pl.BlockSpec(x.shape, lambda i: (0,) * x.ndim)
    return pl.pallas_call(
        _out_kernel,
        grid=(t // rows,),
        in_specs=[row_spec(ATTN_WIDTH), row_spec(HGRN_WIDTH), row_spec(D_MODEL),
                  full(wo), full(lng), full(lnb)],
        out_specs=row_spec(D_MODEL),
        out_shape=jax.ShapeDtypeStruct((t, D_MODEL), F32),
        compiler_params=_params("arbitrary"),
        name=f"out_proj_r{rows}",
    )(a, r, h2d, wo, lng, lnb)


def _block_diag(blocks):
    n, r, c = blocks.shape
    eye = jnp.eye(n, dtype=blocks.dtype)
    return (eye[:, None, :, None] * blocks[:, :, None, :]).reshape(n * r, n * c)


def kernel(x, meta_tokens, rel_bias, hgrn_lb_raw, w_in, kv_norm_g, w_uk, w_uv, hgrn_norm_g, w_out, ln_g, ln_b):
    batch, seq, _ = x.shape
    assert seq % CK == 0 and seq % (HGRN_CHUNK * HGRN_STEP_CHUNKS) == 0 and (batch * seq) % OUT_ROWS == 0
    assert seq // GROUP < WORD_BITS
    k_top = min(TOPK_MAX, seq // 4)

    rel_bias = rel_bias.astype(F32)
    tb, tm = _bias_tables(rel_bias)
    lbraw = hgrn_lb_raw.astype(F32)
    h = x.reshape(batch * seq, D_MODEL).astype(F32)
    hm = meta_tokens.astype(F32)
    s_zero = jnp.zeros((HGRN_HEADS, HGRN_EXPAND, HGRN_EXPAND), F32)

    wp_all = jnp.concatenate(
        [w_in[:, :, :_RAW_SPLIT], jnp.zeros((DEPTH, D_MODEL, _P_GA - _RAW_SPLIT), w_in.dtype),
         w_in[:, :, _RAW_SPLIT:]], axis=2).astype(BF16)

    for l in range(DEPTH):
        wp = wp_all[l]
        wuk_bd = jnp.stack([_block_diag(w_uk[l][2 * p:2 * p + 2])
                            for p in range(ATTN_HEADS // 2)]).astype(BF16)
        wuvt = jnp.stack([_block_diag(w_uv[l][2 * p:2 * p + 2]).T
                          for p in range(ATTN_HEADS // 2)]).astype(BF16)
        wo = w_out[l].astype(BF16)
        kvg = kv_norm_g[l].reshape(1, KV_RANK).astype(F32)
        ng = hgrn_norm_g[l].reshape(1, HGRN_EXPAND).astype(F32)
        lng = ln_g[l].reshape(1, D_MODEL).astype(F32)
        lnb = ln_b[l].reshape(1, D_MODEL).astype(F32)

        hm_pad = jnp.pad(hm, ((0, QB - N_META), (0, 0)))
        (qlt_m, cb_m, cbt_m, _, _, _, ga_m, qf_m, g_m, kk_m, v_m, gh_m) = _project(
            l, hm_pad, wp, wuk_bd, kvg, lbraw, QB)
        cm = cb_m[:N_META]
        cmt = cbt_m[0][:, :N_META]
        a_m = _meta_attention(qlt_m[:, :, :N_META], cm, cmt[:KV_RANK], ga_m[:N_META], tb, wuvt)
        r_m, s_m = _hgrn(1, N_META, N_META, 1, qf_m[:N_META], g_m[:N_META], kk_m[:N_META],
                         v_m[:N_META], gh_m[:N_META], s_zero, ng)

        (qlt, cb, cbt, qit, kib, wt, ga, qf, g, kk, v, gh) = _project(
            l, h, wp, wuk_bd, kvg, lbraw, PROJ_ROWS)
        a = _attention(k_top, batch, seq, qit, kib, wt, qlt, cb, cbt, cm, cmt, ga, tb, tm, wuvt)
        r, _ = _hgrn(batch, seq, HGRN_CHUNK, HGRN_STEP_CHUNKS, qf, g, kk, v, gh, s_m[0], ng)

        hm = _out_project(a_m, r_m, hm, wo, lng, lnb, N_META)
        h = _out_project(a, r, h, wo, lng, lnb, OUT_ROWS)

    return h.reshape(batch, seq, D_MODEL).astype(x.dtype)
```
